```python
import math
import jax, jax.numpy as jnp
from jax import lax
import numpy as np

D_MODEL = 1024
BATCH = 8
SEQ = 4096
DEPTH = 2

N_META = 16
BLOCK = 128
N_PAD = BLOCK - N_META
NEG_INF = -1e30

FOX_HEADS = 8
FOX_HEAD_DIM = 64
FOX_WIDTH = FOX_HEADS * FOX_HEAD_DIM

MLA_HEADS = 8
MLA_NOPE_DIM = 64
MLA_ROPE_DIM = 32
MLA_V_DIM = 64
MLA_Q_RANK = 384
MLA_KV_RANK = 256
MLA_WIDTH = MLA_HEADS * MLA_V_DIM
ROPE_THETA = 10000.0

N_EXPERTS = 16
N_GROUPS = 4
EXPERTS_PER_GROUP = N_EXPERTS // N_GROUPS
TOP_K = 2
D_EXPERT = 256

LN_EPS = 1e-5
RMS_EPS = 1e-6
ALPHA = (2 * DEPTH) ** 0.25
BETA = (8 * DEPTH) ** -0.25

OFF_FOX_Q = FOX_WIDTH
OFF_FOX_K = OFF_FOX_Q + FOX_WIDTH
OFF_FOX_V = OFF_FOX_K + FOX_WIDTH
OFF_FOX_F = OFF_FOX_V + FOX_HEADS
OFF_MLA_CQ = OFF_FOX_F + MLA_Q_RANK
OFF_MLA_CKV = OFF_MLA_CQ + MLA_KV_RANK
OFF_MLA_KR = OFF_MLA_CKV + MLA_ROPE_DIM
OFF_G_FOX = OFF_MLA_KR + D_MODEL
IN_COLS = OFF_G_FOX + D_MODEL
SPLITS = (OFF_FOX_Q, OFF_FOX_K, OFF_FOX_V, OFF_FOX_F, OFF_MLA_CQ, OFF_MLA_CKV, OFF_MLA_KR, OFF_G_FOX)

kernel_name = "fox_mla_gated_hybrid_moe_deepnorm"


def layer_norm(x, g, b):
    xf = x.astype(jnp.float32)
    mu = jnp.mean(xf, axis=-1, keepdims=True)
    var = jnp.mean(jnp.square(xf - mu), axis=-1, keepdims=True)
    return ((xf - mu) * lax.rsqrt(var + LN_EPS) * g.astype(jnp.float32) + b.astype(jnp.float32)).astype(x.dtype)


def rms_norm(x, g):
    xf = x.astype(jnp.float32)
    ms = jnp.mean(jnp.square(xf), axis=-1, keepdims=True)
    return (xf * lax.rsqrt(ms + RMS_EPS) * g.astype(jnp.float32)).astype(x.dtype)


def rope(x, pos):
    half = x.shape[-1] // 2
    inv = ROPE_THETA ** (-jnp.arange(half, dtype=jnp.float32) / half)
    ang = pos.astype(jnp.float32)[:, None] * inv[None, :]
    cos = jnp.cos(ang)[:, None, :]
    sin = jnp.sin(ang)[:, None, :]
    xf = x.astype(jnp.float32)
    x1, x2 = xf[..., :half], xf[..., half:]
    return jnp.concatenate([x1 * cos - x2 * sin, x2 * cos + x1 * sin], axis=-1).astype(x.dtype)


def blocked_causal_attention(q, k, v, scale, decay=None):
    lp = q.shape[2]
    outs = []
    for i in range(lp // BLOCK):
        q0, q1 = i * BLOCK, (i + 1) * BLOCK
        s = jnp.einsum("bhqd,bhkd->bhqk", q[:, :, q0:q1], k[:, :, :q1],
                       preferred_element_type=jnp.float32) * scale
        if decay is not None:
            s = s + decay[:, :, q0:q1, None] - decay[:, :, None, :q1]
        qi = jnp.arange(q0, q1)[:, None]
        ki = jnp.arange(q1)[None, :]
        s = jnp.where((ki <= qi) & (ki >= N_PAD), s, NEG_INF)
        p = jax.nn.softmax(s, axis=-1)
        outs.append(jnp.einsum("bhqk,bhkd->bhqd", p.astype(v.dtype), v[:, :, :q1]))
    return jnp.concatenate(outs, axis=2)


def _pad_seq(t):
    widths = [(0, 0)] * t.ndim
    widths[2] = (N_PAD, 0)
    return jnp.pad(t, widths)


def hybrid_mixer(h, pos, w_in, fox_f_bias, fox_w_o, mla_q_norm, mla_w_uq, mla_kv_norm, mla_w_ukv,
                 mla_w_o, w_out):
    b, l, _ = h.shape
    proj = h @ w_in
    q_f, k_f, v_f, f_logit, c_q, c_kv, k_r, g_fox, g_mla = jnp.split(proj, SPLITS, axis=-1)

    def to_heads(t, n_heads):
        return t.reshape(b, l, n_heads, -1).transpose(0, 2, 1, 3)

    log_f = jax.nn.log_sigmoid(f_logit.astype(jnp.float32) + fox_f_bias.astype(jnp.float32))
    decay = jnp.cumsum(log_f, axis=1).transpose(0, 2, 1)
    decay = jnp.pad(decay, ((0, 0), (0, 0), (N_PAD, 0)))
    o_fox = blocked_causal_attention(_pad_seq(to_heads(q_f, FOX_HEADS)), _pad_seq(to_heads(k_f, FOX_HEADS)),
                                     _pad_seq(to_heads(v_f, FOX_HEADS)), FOX_HEAD_DIM ** -0.5, decay)
    o_fox = o_fox[:, :, N_PAD:].transpose(0, 2, 1, 3).reshape(b, l, FOX_WIDTH)
    y_fox = o_fox @ fox_w_o

    q_all = (rms_norm(c_q, mla_q_norm) @ mla_w_uq).reshape(b, l, MLA_HEADS, MLA_NOPE_DIM + MLA_ROPE_DIM)
    q_nope, q_rot = q_all[..., :MLA_NOPE_DIM], rope(q_all[..., MLA_NOPE_DIM:], pos)
    kv = (rms_norm(c_kv, mla_kv_norm) @ mla_w_ukv).reshape(b, l, MLA_HEADS, MLA_NOPE_DIM + MLA_V_DIM)
    k_nope, v_m = kv[..., :MLA_NOPE_DIM], kv[..., MLA_NOPE_DIM:]
    k_rot = jnp.broadcast_to(rope(k_r[:, :, None, :], pos), (b, l, MLA_HEADS, MLA_ROPE_DIM))
    q_m = jnp.concatenate([q_nope, q_rot], axis=-1).transpose(0, 2, 1, 3)
    k_m = jnp.concatenate([k_nope, k_rot], axis=-1).transpose(0, 2, 1, 3)
    v_m = v_m.transpose(0, 2, 1, 3)
    o_mla = blocked_causal_attention(_pad_seq(q_m), _pad_seq(k_m), _pad_seq(v_m),
                                     (MLA_NOPE_DIM + MLA_ROPE_DIM) ** -0.5)
    o_mla = o_mla[:, :, N_PAD:].transpose(0, 2, 1, 3).reshape(b, l, MLA_WIDTH)
    y_mla = o_mla @ mla_w_o

    merged = jax.nn.sigmoid(g_fox) * y_fox + jax.nn.sigmoid(g_mla) * y_mla
    return merged @ w_out


def grouped_moe(x, router_w, router_b, w_gate, w_up, w_down):
    b, l, d = x.shape
    t = x.reshape(b * l, d)
    scores = jax.nn.sigmoid((t @ router_w).astype(jnp.float32))
    biased = (scores + router_b.astype(jnp.float32)).reshape(-1, N_GROUPS, EXPERTS_PER_GROUP)
    group_score = jnp.sum(lax.top_k(biased, TOP_K)[0], axis=-1)
    g_sel = jnp.argmax(group_score, axis=-1)
    in_group = jnp.take_along_axis(biased, g_sel[:, None, None], axis=1)[:, 0]
    _, local = lax.top_k(in_group, TOP_K)
    expert_idx = g_sel[:, None] * EXPERTS_PER_GROUP + local
    sel = jnp.take_along_axis(scores, expert_idx, axis=1)
    gates = sel / jnp.sum(sel, axis=-1, keepdims=True)
    combine = jnp.sum(jax.nn.one_hot(expert_idx, N_EXPERTS, dtype=jnp.float32) * gates[..., None], axis=1)
    combine = combine.astype(t.dtype)
    y = jnp.zeros_like(t)
    for e in range(N_EXPERTS):
        hid = jax.nn.silu(t @ w_gate[e]) * (t @ w_up[e])
        y = y + combine[:, e:e + 1] * (hid @ w_down[e])
    return y.reshape(b, l, d)


def setup_inputs(seed: int = 0) -> dict:
    key = jax.random.key(seed)
    ks = jax.random.split(key, 24)
    f32 = jnp.float32
    nrm = lambda k, shape, scale: jax.random.normal(k, shape, f32) * scale
    gain = lambda k, shape: 1.0 + 0.02 * jax.random.normal(k, shape, f32)
    return {
        "x": jax.random.normal(ks[0], (BATCH, SEQ, D_MODEL), f32),
        "meta_tokens": nrm(ks[1], (N_META, D_MODEL), 1.0),
        "ln_in_g": gain(ks[2], (D_MODEL,)),
        "ln_in_b": nrm(ks[3], (D_MODEL,), 0.02),
        "w_in": nrm(ks[4], (DEPTH, D_MODEL, IN_COLS), D_MODEL ** -0.5),
        "fox_f_bias": 2.0 + nrm(ks[5], (DEPTH, FOX_HEADS), 0.1),
        "fox_w_o": nrm(ks[6], (DEPTH, FOX_WIDTH, D_MODEL), BETA * FOX_WIDTH ** -0.5),
        "mla_q_norm": gain(ks[7], (DEPTH, MLA_Q_RANK)),
        "mla_w_uq": nrm(ks[8], (DEPTH, MLA_Q_RANK, MLA_HEADS * (MLA_NOPE_DIM + MLA_ROPE_DIM)), MLA_Q_RANK ** -0.5),
        "mla_kv_norm": gain(ks[9], (DEPTH, MLA_KV_RANK)),
        "mla_w_ukv": nrm(ks[10], (DEPTH, MLA_KV_RANK, MLA_HEADS * (MLA_NOPE_DIM + MLA_V_DIM)), MLA_KV_RANK ** -0.5),
        "mla_w_o": nrm(ks[11], (DEPTH, MLA_WIDTH, D_MODEL), BETA * MLA_WIDTH ** -0.5),
        "w_out": nrm(ks[12], (DEPTH, D_MODEL, D_MODEL), BETA * D_MODEL ** -0.5),
        "ln1_g": gain(ks[13], (DEPTH, D_MODEL)),
        "ln1_b": nrm(ks[14], (DEPTH, D_MODEL), 0.02),
        "router_w": nrm(ks[15], (D_MODEL, N_EXPERTS), D_MODEL ** -0.5),
        "router_b": nrm(ks[16], (N_EXPERTS,), 0.01),
        "w_gate": nrm(ks[17], (DEPTH, N_EXPERTS, D_MODEL, D_EXPERT), D_MODEL ** -0.5),
        "w_up": nrm(ks[18], (DEPTH, N_EXPERTS, D_MODEL, D_EXPERT), D_MODEL ** -0.5),
        "w_down": nrm(ks[19], (DEPTH, N_EXPERTS, D_EXPERT, D_MODEL), BETA * D_EXPERT ** -0.5),
        "ln2_g": gain(ks[20], (DEPTH, D_MODEL)),
        "ln2_b": nrm(ks[21], (DEPTH, D_MODEL), 0.02),
    }


def reference(x, meta_tokens, ln_in_g, ln_in_b, w_in, fox_f_bias, fox_w_o, mla_q_norm, mla_w_uq,
              mla_kv_norm, mla_w_ukv, mla_w_o, w_out, ln1_g, ln1_b, router_w, router_b,
              w_gate, w_up, w_down, ln2_g, ln2_b):
    b = x.shape[0]
    meta = jnp.broadcast_to(meta_tokens[None].astype(x.dtype), (b, N_META, D_MODEL))
    h = jnp.concatenate([meta, x], axis=1)
    pos = jnp.arange(h.shape[1], dtype=jnp.int32)
    h = layer_norm(h, ln_in_g, ln_in_b)
    for i in range(DEPTH):
        mix = hybrid_mixer(h, pos, w_in[i], fox_f_bias[i], fox_w_o[i], mla_q_norm[i], mla_w_uq[i],
                           mla_kv_norm[i], mla_w_ukv[i], mla_w_o[i], w_out[i])
        h = layer_norm(ALPHA * h + mix, ln1_g[i], ln1_b[i])
        ffn = grouped_moe(h, router_w, router_b, w_gate[i], w_up[i], w_down[i])
        h = layer_norm(ALPHA * h + ffn, ln2_g[i], ln2_b[i])
    return h[:, N_META:]
```

```python
import functools
import math

import jax
import jax.numpy as jnp
from jax import lax
from jax.experimental import pallas as pl
from jax.experimental.pallas import tpu as pltpu

F32 = jnp.float32
BF16 = jnp.bfloat16

D_MODEL = 1024
N_META = 16
BLOCK = 128
N_PAD = BLOCK - N_META
NEG_INF = -1e30

FOX_HEADS = 8
FOX_HEAD_DIM = 64
FOX_WIDTH = FOX_HEADS * FOX_HEAD_DIM

MLA_HEADS = 8
MLA_NOPE_DIM = 64
MLA_ROPE_DIM = 32
MLA_V_DIM = 64
MLA_Q_RANK = 384
MLA_KV_RANK = 256
MLA_WIDTH = MLA_HEADS * MLA_V_DIM
ROPE_THETA = 10000.0

N_EXPERTS = 16
N_GROUPS = 4
EXPERTS_PER_GROUP = N_EXPERTS // N_GROUPS
D_EXPERT = 256

LN_EPS = 1e-5
RMS_EPS = 1e-6

LANES = 128
HEAD_PAIRS = FOX_HEADS // 2
F_LANE = 0
KR_LANE = 64

C_Q, C_K, C_V = 0, FOX_WIDTH, 2 * FOX_WIDTH
C_CQ = 3 * FOX_WIDTH
C_CKV = C_CQ + MLA_Q_RANK
C_G = C_CKV + MLA_KV_RANK
C_MISC = C_G + 2 * D_MODEL
PROJ_COLS = C_MISC + LANES

VMEM_LIMIT = 56 * 1024 * 1024

TM_PROJ = 512
TM_PREP = 384
TM_POST = 256
TM_MOE = 512
T_ATTN = 384
T_DECAY = 384
MOE_CHUNKS = 4


def _cparams(sem):
    return pltpu.CompilerParams(dimension_semantics=sem, vmem_limit_bytes=VMEM_LIMIT)


def _layer_norm(x, g, b):
    mu = jnp.mean(x, axis=-1, keepdims=True)
    xc = x - mu
    var = jnp.mean(xc * xc, axis=-1, keepdims=True)
    return xc * lax.rsqrt(var + LN_EPS) * g + b


def _rms_norm(x, g):
    ms = jnp.mean(x * x, axis=-1, keepdims=True)
    return x * lax.rsqrt(ms + RMS_EPS) * g


def _ln_in_kernel(x_ref, meta_ref, g_ref, b_ref, o_ref):
    i = pl.program_id(1)

    @pl.when(i == 0)
    def _():
        o_ref[0:N_PAD, :] = jnp.zeros((N_PAD, D_MODEL), F32)
        o_ref[N_PAD:, :] = _layer_norm(meta_ref[...], g_ref[...], b_ref[...])

    @pl.when(i > 0)
    def _():
        o_ref[...] = _layer_norm(x_ref[...], g_ref[...], b_ref[...])


def _ln_in(x2d, meta, g, b, batch, seq):
    nb_in = seq // BLOCK
    nb_out = nb_in + 1
    return pl.pallas_call(
        _ln_in_kernel,
        out_shape=jax.ShapeDtypeStruct((batch * nb_out * BLOCK, D_MODEL), F32),
        grid=(batch, nb_out),
        in_specs=[
            pl.BlockSpec((BLOCK, D_MODEL), lambda bb, i: (bb * nb_in + jnp.maximum(i - 1, 0), 0)),
            pl.BlockSpec((N_META, D_MODEL), lambda bb, i: (0, 0)),
            pl.BlockSpec((1, D_MODEL), lambda bb, i: (0, 0)),
            pl.BlockSpec((1, D_MODEL), lambda bb, i: (0, 0)),
        ],
        out_specs=pl.BlockSpec((BLOCK, D_MODEL), lambda bb, i: (bb * nb_out + i, 0)),
        compiler_params=_cparams(("parallel", "arbitrary")),
        name="ln_in",
    )(x2d, meta, g, b)


def _proj_kernel(x_ref, w_ref, q_ref, k_ref, v_ref, cq_ref, ckv_ref, g_ref, misc_ref):
    x = x_ref[...].astype(BF16)

    def mm(lo, hi):
        return jnp.dot(x, w_ref[:, lo:hi], preferred_element_type=F32)

    q_ref[...] = (mm(C_Q, C_K) * (FOX_HEAD_DIM ** -0.5)).astype(BF16)
    k_ref[...] = mm(C_K, C_V).astype(BF16)
    v_ref[...] = mm(C_V, C_CQ).astype(BF16)
    cq_ref[...] = mm(C_CQ, C_CKV)
    ckv_ref[...] = mm(C_CKV, C_G)
    g_ref[:, :D_MODEL] = mm(C_G, C_G + D_MODEL)
    g_ref[:, D_MODEL:] = mm(C_G + D_MODEL, C_MISC)
    misc_ref[...] = mm(C_MISC, PROJ_COLS)


def _proj(h, w_big):
    rows = h.shape[0]
    tm = TM_PROJ
    row = lambda i: (i, 0)
    return pl.pallas_call(
        _proj_kernel,
        out_shape=(
            jax.ShapeDtypeStruct((rows, FOX_WIDTH), BF16),
            jax.ShapeDtypeStruct((rows, FOX_WIDTH), BF16),
            jax.ShapeDtypeStruct((rows, FOX_WIDTH), BF16),
            jax.ShapeDtypeStruct((rows, MLA_Q_RANK), F32),
            jax.ShapeDtypeStruct((rows, MLA_KV_RANK), F32),
            jax.ShapeDtypeStruct((rows, 2 * D_MODEL), F32),
            jax.ShapeDtypeStruct((rows, LANES), F32),
        ),
        grid=(rows // tm,),
        in_specs=[
            pl.BlockSpec((tm, D_MODEL), row),
            pl.BlockSpec((D_MODEL, PROJ_COLS), lambda i: (0, 0)),
        ],
        out_specs=(
            pl.BlockSpec((tm, FOX_WIDTH), row),
            pl.BlockSpec((tm, FOX_WIDTH), row),
            pl.BlockSpec((tm, FOX_WIDTH), row),
            pl.BlockSpec((tm, MLA_Q_RANK), row),
            pl.BlockSpec((tm, MLA_KV_RANK), row),
            pl.BlockSpec((tm, 2 * D_MODEL), row),
            pl.BlockSpec((tm, LANES), row),
        ),
        compiler_params=_cparams(("parallel",)),
        name="in_proj",
    )(h, w_big)


def _decay_kernel(misc_ref, bias_ref, ccol_ref, crow_ref, carry_ref):
    j = pl.program_id(1)
    t = misc_ref.shape[0]

    @pl.when(j == 0)
    def _():
        carry_ref[...] = jnp.zeros_like(carry_ref)

    z = misc_ref[...] + bias_ref[...]
    logf = jnp.minimum(z, 0.0) - jnp.log1p(jnp.exp(-jnp.abs(z)))
    row = lax.broadcasted_iota(jnp.int32, (t, LANES), 0)
    logf = jnp.where(row + j * t >= N_PAD, logf, 0.0)
    c = logf
    shift = 1
    while shift < t:
        c = c + jnp.where(row >= shift, pltpu.roll(c, shift, 0), 0.0)
        shift *= 2
    c = c + carry_ref[...]
    carry_ref[...] = c[t - 1:t, :]
    ccol_ref[...] = c
    crow_ref[0] = c.T[0:FOX_HEADS, :]


def _decay(misc, bias_row, batch, lp):
    t = T_DECAY
    nb = lp // t
    return pl.pallas_call(
        _decay_kernel,
        out_shape=(
            jax.ShapeDtypeStruct((batch * lp, LANES), F32),
            jax.ShapeDtypeStruct((batch, FOX_HEADS, lp), F32),
        ),
        grid=(batch, nb),
        in_specs=[
            pl.BlockSpec((t, LANES), lambda b, j: (b * nb + j, 0)),
            pl.BlockSpec((1, LANES), lambda b, j: (0, 0)),
        ],
        out_specs=(
            pl.BlockSpec((t, LANES), lambda b, j: (b * nb + j, 0)),
            pl.BlockSpec((1, FOX_HEADS, t), lambda b, j: (b, 0, j)),
        ),
        scratch_shapes=[pltpu.VMEM((1, LANES), F32)],
        compiler_params=_cparams(("parallel", "arbitrary")),
        name="fox_decay",
    )(misc, bias_row)


def _attn_kernel(*refs, t, qk_w, decay):
    if decay:
        q_ref, k_ref, v_ref, ccol_ref, crow_ref, o_ref = refs
    else:
        q_ref, k_ref, v_ref, o_ref = refs
    hp = pl.program_id(1)
    qi = pl.program_id(2)
    nq = pl.num_programs(2)

    q = q_ref[...]
    lane = lax.broadcasted_iota(jnp.int32, (t, LANES), 1)
    low = lane < FOX_HEAD_DIM
    if qk_w == LANES:
        keep_lo = jnp.where(low, 1.0, 0.0).astype(BF16)
        qs = (q * keep_lo, q * (1.0 - keep_lo))
    else:
        qs = (q[:, :LANES], q[:, LANES:])
    if decay:
        cc = ccol_ref[...]
        cts = [jnp.sum(jnp.where(lane == 2 * hp + jj, cc, 0.0), axis=1, keepdims=True)
               for jj in range(2)]
    q_pos = qi * t + lax.broadcasted_iota(jnp.int32, (t, t), 0)
    nt_dims = (((1,), (1,)), ((), ()))

    def step(kj, carry, masked):
        start = pl.multiple_of(kj * t, t)
        k = k_ref[pl.ds(start, t), :]
        v = v_ref[pl.ds(start, t), :]
        if masked:
            k_pos = kj * t + lax.broadcasted_iota(jnp.int32, (t, t), 1)
            mask = (k_pos <= q_pos) & (k_pos >= N_PAD)
        out = []
        for jj in range(2):
            m, l, acc = carry[jj]
            kh = k if qk_w == LANES else k[:, jj * LANES:(jj + 1) * LANES]
            s = lax.dot_general(qs[jj], kh, nt_dims, preferred_element_type=F32)
            if decay:
                cs = crow_ref[0, pl.ds((2 * hp + jj) * nq + kj, 1), :]
                s = s + (cts[jj] - cs)
            if masked:
                s = jnp.where(mask, s, NEG_INF)
            m_new = jnp.maximum(m, jnp.max(s, axis=1, keepdims=True))
            alpha = jnp.exp(m - m_new)
            p = jnp.exp(s - m_new)
            l_new = alpha * l + jnp.sum(p, axis=1, keepdims=True)
            acc_new = alpha * acc + jnp.dot(p.astype(BF16), v, preferred_element_type=F32)
            out.append((m_new, l_new, acc_new))
        return tuple(out)

    init_one = (jnp.full((t, 1), NEG_INF, F32), jnp.zeros((t, 1), F32), jnp.zeros((t, LANES), F32))
    carry = (init_one, init_one)
    carry = step(0, carry, True)
    carry = lax.fori_loop(1, qi, lambda kj, c: step(kj, c, False), carry)
    carry = lax.fori_loop(jnp.maximum(qi, 1), qi + 1, lambda kj, c: step(kj, c, True), carry)
    (_, l0, a0), (_, l1, a1) = carry
    o_ref[...] = jnp.where(low, a0 / l0, a1 / l1).astype(o_ref.dtype)


def _attention(q, k, v, batch, lp, qk_w, decay=None):
    t = T_ATTN
    nq = lp // t
    kern = functools.partial(_attn_kernel, t=t, qk_w=qk_w, decay=decay is not None)
    in_specs = [
        pl.BlockSpec((t, qk_w), lambda b, hp, qi: (b * nq + qi, hp)),
        pl.BlockSpec((lp, qk_w), lambda b, hp, qi: (b, hp)),
        pl.BlockSpec((lp, LANES), lambda b, hp, qi: (b, hp)),
    ]
    args = [q, k, v]
    if decay is not None:
        ccol, crow = decay
        in_specs += [
            pl.BlockSpec((t, LANES), lambda b, hp, qi: (b * nq + qi, 0)),
            pl.BlockSpec((1, FOX_HEADS * nq, t), lambda b, hp, qi: (b, 0, 0)),
        ]
        args += [ccol, crow.reshape(batch, FOX_HEADS * nq, t)]
    return pl.pallas_call(
        kern,
        out_shape=jax.ShapeDtypeStruct((batch * lp, HEAD_PAIRS * LANES), BF16),
        grid=(batch, HEAD_PAIRS, nq),
        in_specs=in_specs,
        out_specs=pl.BlockSpec((t, LANES), lambda b, hp, qi: (b * nq + qi, hp)),
        compiler_params=_cparams(("parallel", "parallel", "arbitrary")),
        name="fox_attn" if decay is not None else "mla_attn",
    )(*args)


def _rope_lanes(x, cos, sin, first_half):
    w = x.shape[1]
    from_hi = pltpu.roll(x, w - MLA_ROPE_DIM // 2, 1)
    from_lo = pltpu.roll(x, MLA_ROPE_DIM // 2, 1)
    swapped = jnp.where(first_half, -from_hi, from_lo)
    return x * cos + swapped * sin


def _mla_prep_kernel(cq_ref, ckv_ref, misc_ref, cos_ref, sin_ref, qn_ref, kvn_ref,
                     wuq_ref, wuk_ref, wuv_ref, q_ref, k_ref, v_ref):
    tm = cq_ref.shape[0]
    cos1, sin1 = cos_ref[...], sin_ref[...]
    lane1 = lax.broadcasted_iota(jnp.int32, (tm, LANES), 1)
    first1 = lane1 < KR_LANE + MLA_ROPE_DIM // 2
    rope1 = (lane1 >= KR_LANE) & (lane1 < KR_LANE + MLA_ROPE_DIM)
    cos8 = jnp.concatenate([cos1] * MLA_HEADS, axis=1)
    sin8 = jnp.concatenate([sin1] * MLA_HEADS, axis=1)
    first8 = jnp.concatenate([first1] * MLA_HEADS, axis=1)

    qn = _rms_norm(cq_ref[...], qn_ref[...]).astype(BF16)
    q = jnp.dot(qn, wuq_ref[...], preferred_element_type=F32)
    q = _rope_lanes(q, cos8, sin8, first8) * ((MLA_NOPE_DIM + MLA_ROPE_DIM) ** -0.5)
    q_ref[...] = q.astype(BF16)

    kvn = _rms_norm(ckv_ref[...], kvn_ref[...]).astype(BF16)
    k_nope = jnp.dot(kvn, wuk_ref[...], preferred_element_type=F32)
    k_rot = _rope_lanes(jnp.where(rope1, misc_ref[...], 0.0), cos1, sin1, first1)
    k_ref[...] = (k_nope + jnp.concatenate([k_rot] * MLA_HEADS, axis=1)).astype(BF16)
    v_ref[...] = jnp.dot(kvn, wuv_ref[...], preferred_element_type=F32).astype(BF16)


def _mla_prep(cq, ckv, misc, cos_t, sin_t, qn, kvn, wuq, wuk, wuv, lp):
    rows = cq.shape[0]
    tm = TM_PREP
    assert lp % tm == 0
    nb = lp // tm
    row = lambda i: (i, 0)
    pos = lambda i: (i % nb, 0)
    full = lambda i: (0, 0)
    wide = MLA_HEADS * LANES
    return pl.pallas_call(
        _mla_prep_kernel,
        out_shape=(
            jax.ShapeDtypeStruct((rows, wide), BF16),
            jax.ShapeDtypeStruct((rows, wide), BF16),
            jax.ShapeDtypeStruct((rows, MLA_WIDTH), BF16),
        ),
        grid=(rows // tm,),
        in_specs=[
            pl.BlockSpec((tm, MLA_Q_RANK), row),
            pl.BlockSpec((tm, MLA_KV_RANK), row),
            pl.BlockSpec((tm, LANES), row),
            pl.BlockSpec((tm, LANES), pos),
            pl.BlockSpec((tm, LANES), pos),
            pl.BlockSpec((1, MLA_Q_RANK), full),
            pl.BlockSpec((1, MLA_KV_RANK), full),
            pl.BlockSpec((MLA_Q_RANK, wide), full),
            pl.BlockSpec((MLA_KV_RANK, wide), full),
            pl.BlockSpec((MLA_KV_RANK, MLA_WIDTH), full),
        ],
        out_specs=(
            pl.BlockSpec((tm, wide), row),
            pl.BlockSpec((tm, wide), row),
            pl.BlockSpec((tm, MLA_WIDTH), row),
        ),
        compiler_params=_cparams(("parallel",)),
        name="mla_prep",
    )(cq, ckv, misc, cos_t, sin_t, qn, kvn, wuq, wuk, wuv)


def _top2_sum(a, b, c, d):
    hi1, lo1 = jnp.maximum(a, b), jnp.minimum(a, b)
    hi2, lo2 = jnp.maximum(c, d), jnp.minimum(c, d)
    return jnp.maximum(hi1, hi2) + jnp.maximum(jnp.minimum(hi1, hi2), jnp.maximum(lo1, lo2))


def _route(logits_t, bias_col):
    scores = jax.nn.sigmoid(logits_t)
    biased = scores + bias_col
    b = [biased[e:e + 1, :] for e in range(N_EXPERTS)]
    s = [scores[e:e + 1, :] for e in range(N_EXPERTS)]
    gscore = [_top2_sum(*b[EXPERTS_PER_GROUP * g:EXPERTS_PER_GROUP * (g + 1)]) for g in range(N_GROUPS)]
    best = gscore[0]
    gidx = jnp.zeros_like(best, dtype=jnp.int32)
    for g in range(1, N_GROUPS):
        better = gscore[g] > best
        gidx = jnp.where(better, g, gidx)
        best = jnp.where(better, gscore[g], best)
    in_g = [gidx == g for g in range(N_GROUPS)]

    def pick(vals, j):
        out = vals[j]
        for g in range(1, N_GROUPS):
            out = jnp.where(in_g[g], vals[EXPERTS_PER_GROUP * g + j], out)
        return out

    vb = [pick(b, j) for j in range(EXPERTS_PER_GROUP)]
    vs = [pick(s, j) for j in range(EXPERTS_PER_GROUP)]
    chosen = []
    for j in range(EXPERTS_PER_GROUP):
        rank = jnp.zeros_like(gidx)
        for i in range(EXPERTS_PER_GROUP):
            if i == j:
                continue
            ahead = (vb[i] >= vb[j]) if i < j else (vb[i] > vb[j])
            rank = rank + jnp.where(ahead, 1, 0)
        chosen.append(rank < 2)
    total = sum(jnp.where(chosen[j], vs[j], 0.0) for j in range(EXPERTS_PER_GROUP))
    gates = [jnp.where(chosen[j], vs[j] / total, 0.0) for j in range(EXPERTS_PER_GROUP)]
    rows = []
    for g in range(N_GROUPS):
        for j in range(EXPERTS_PER_GROUP):
            rows.append(jnp.where(in_g[g], gates[j], 0.0))
    return jnp.concatenate(rows, axis=0)


def _post_kernel(of_ref, om_ref, g_ref, h_ref, wfo_ref, wmo_ref, wout_ref, lng_ref, lnb_ref,
                 rwh_ref, rwl_ref, rb_ref, h1_ref, comb_ref, *, alpha):
    y_fox = jnp.dot(of_ref[...], wfo_ref[...], preferred_element_type=F32)
    y_mla = jnp.dot(om_ref[...], wmo_ref[...], preferred_element_type=F32)
    merged = (jax.nn.sigmoid(g_ref[:, :D_MODEL]) * y_fox
              + jax.nn.sigmoid(g_ref[:, D_MODEL:]) * y_mla)
    mix = jnp.dot(merged.astype(BF16), wout_ref[...], preferred_element_type=F32)
    h1 = _layer_norm(alpha * h_ref[...] + mix, lng_ref[...], lnb_ref[...])
    h1_ref[...] = h1
    h_hi = h1.astype(BF16)
    h_lo = (h1 - h_hi.astype(F32)).astype(BF16)
    nt_dims = (((1,), (1,)), ((), ()))
    rwh, rwl = rwh_ref[...], rwl_ref[...]
    logits_t = (lax.dot_general(rwh, h_hi, nt_dims, preferred_element_type=F32)
                + lax.dot_general(rwl, h_hi, nt_dims, preferred_element_type=F32)
                + lax.dot_general(rwh, h_lo, nt_dims, preferred_element_type=F32))
    comb_ref[...] = _route(logits_t, rb_ref[...])


def _post(o_fox, o_mla, g, h, wfo, wmo, wout, lng, lnb, rwh, rwl, rb, alpha):
    rows = h.shape[0]
    tm = TM_POST
    row = lambda i: (i, 0)
    full = lambda i: (0, 0)
    return pl.pallas_call(
        functools.partial(_post_kernel, alpha=alpha),
        out_shape=(
            jax.ShapeDtypeStruct((rows, D_MODEL), F32),
            jax.ShapeDtypeStruct((N_EXPERTS, rows), F32),
        ),
        grid=(rows // tm,),
        in_specs=[
            pl.BlockSpec((tm, FOX_WIDTH), row),
            pl.BlockSpec((tm, MLA_WIDTH), row),
            pl.BlockSpec((tm, 2 * D_MODEL), row),
            pl.BlockSpec((tm, D_MODEL), row),
            pl.BlockSpec((FOX_WIDTH, D_MODEL), full),
            pl.BlockSpec((MLA_WIDTH, D_MODEL), full),
            pl.BlockSpec((D_MODEL, D_MODEL), full),
            pl.BlockSpec((1, D_MODEL), full),
            pl.BlockSpec((1, D_MODEL), full),
            pl.BlockSpec((N_EXPERTS, D_MODEL), full),
            pl.BlockSpec((N_EXPERTS, D_MODEL), full),
            pl.BlockSpec((N_EXPERTS, 1), full),
        ],
        out_specs=(
            pl.BlockSpec((tm, D_MODEL), row),
            pl.BlockSpec((N_EXPERTS, tm), lambda i: (0, i)),
        ),
        compiler_params=_cparams(("parallel",)),
        name="merge_ln1_router",
    )(o_fox, o_mla, g, h, wfo, wmo, wout, lng, lnb, rwh, rwl, rb)


def _moe_kernel(h_ref, comb_ref, wg_ref, wu_ref, wd_ref, lng_ref, lnb_ref, o_ref,
                xb_ref, acc_ref, cw_ref, *, alpha):
    c = pl.program_id(1)
    tm = h_ref.shape[0]

    @pl.when(c == 0)
    def _():
        xb_ref[...] = h_ref[...].astype(BF16)
        acc_ref[...] = jnp.zeros_like(acc_ref)
        padded = jnp.concatenate([comb_ref[...], jnp.zeros((LANES - N_EXPERTS, tm), F32)], axis=0)
        cw_ref[...] = padded.T

    x = xb_ref[...]
    gate = jnp.dot(x, wg_ref[...], preferred_element_type=F32)
    up = jnp.dot(x, wu_ref[...], preferred_element_type=F32)
    hid = gate * jax.nn.sigmoid(gate) * up
    cw = cw_ref[...]
    lane = lax.broadcasted_iota(jnp.int32, (tm, LANES), 1)
    pieces = []
    for e in range(EXPERTS_PER_GROUP):
        w_e = jnp.sum(jnp.where(lane == c * EXPERTS_PER_GROUP + e, cw, 0.0), axis=1, keepdims=True)
        pieces.append((hid[:, e * D_EXPERT:(e + 1) * D_EXPERT] * w_e).astype(BF16))
    hid_w = jnp.concatenate(pieces, axis=1)
    acc_ref[...] += jnp.dot(hid_w, wd_ref[...], preferred_element_type=F32)

    @pl.when(c == MOE_CHUNKS - 1)
    def _():
        o_ref[...] = _layer_norm(alpha * h_ref[...] + acc_ref[...], lng_ref[...], lnb_ref[...])


def _moe(h1, comb_t, wg, wu, wd, lng, lnb, alpha):
    rows = h1.shape[0]
    tm = TM_MOE
    chunk = EXPERTS_PER_GROUP * D_EXPERT
    return pl.pallas_call(
        functools.partial(_moe_kernel, alpha=alpha),
        out_shape=jax.ShapeDtypeStruct((rows, D_MODEL), F32),
        grid=(rows // tm, MOE_CHUNKS),
        in_specs=[
            pl.BlockSpec((tm, D_MODEL), lambda i, c: (i, 0)),
            pl.BlockSpec((N_EXPERTS, tm), lambda i, c: (0, i)),
            pl.BlockSpec((D_MODEL, chunk), lambda i, c: (0, c)),
            pl.BlockSpec((D_MODEL, chunk), lambda i, c: (0, c)),
            pl.BlockSpec((chunk, D_MODEL), lambda i, c: (c, 0)),
            pl.BlockSpec((1, D_MODEL), lambda i, c: (0, 0)),
            pl.BlockSpec((1, D_MODEL), lambda i, c: (0, 0)),
        ],
        out_specs=pl.BlockSpec((tm, D_MODEL), lambda i, c: (i, 0)),
        scratch_shapes=[
            pltpu.VMEM((tm, D_MODEL), BF16),
            pltpu.VMEM((tm, D_MODEL), F32),
            pltpu.VMEM((tm, LANES), F32),
        ],
        compiler_params=_cparams(("parallel", "arbitrary")),
        name="moe_ln2",
    )(h1, comb_t, wg, wu, wd, lng, lnb)


def _rope_tables(lp):
    half = MLA_ROPE_DIM // 2
    pos = jnp.maximum(jnp.arange(lp, dtype=jnp.int32) - N_PAD, 0).astype(F32)
    inv = ROPE_THETA ** (-jnp.arange(half, dtype=F32) / half)
    ang = pos[:, None] * inv[None, :]
    cos, sin = jnp.cos(ang), jnp.sin(ang)
    ones = jnp.ones((lp, KR_LANE), F32)
    tail = LANES - KR_LANE - MLA_ROPE_DIM
    cos_t = jnp.concatenate([ones, cos, cos, jnp.ones((lp, tail), F32)], axis=1)
    sin_t = jnp.concatenate([0 * ones, sin, sin, jnp.zeros((lp, tail), F32)], axis=1)
    return cos_t, sin_t


def _pad_heads(w, n_heads, per_head, keep_lo, keep_hi):
    k = w.shape[0]
    w = w.reshape(k, n_heads, per_head)[:, :, keep_lo:keep_hi]
    w = jnp.pad(w, ((0, 0), (0, 0), (0, LANES - (keep_hi - keep_lo))))
    return w.reshape(k, n_heads * LANES)


def kernel(x, meta_tokens, ln_in_g, ln_in_b, w_in, fox_f_bias, fox_w_o, mla_q_norm, mla_w_uq,
           mla_kv_norm, mla_w_ukv, mla_w_o, w_out, ln1_g, ln1_b, router_w, router_b,
           w_gate, w_up, w_down, ln2_g, ln2_b):
    batch, seq, _ = x.shape
    depth = w_in.shape[0]
    lp = seq + BLOCK
    alpha = (2 * depth) ** 0.25
    row = lambda a: a.reshape(1, -1).astype(F32)

    h = _ln_in(x.reshape(batch * seq, D_MODEL), meta_tokens.astype(F32), row(ln_in_g), row(ln_in_b),
               batch, seq)
    cos_t, sin_t = _rope_tables(lp)
    rw_t = router_w.T.astype(F32)
    rw_hi = rw_t.astype(BF16)
    rw_lo = (rw_t - rw_hi.astype(F32)).astype(BF16)
    rb = router_b.reshape(N_EXPERTS, 1).astype(F32)

    o0, o1, o2, o3, o4, o5, o6, o7 = (FOX_WIDTH, 2 * FOX_WIDTH, 3 * FOX_WIDTH,
                                      3 * FOX_WIDTH + FOX_HEADS,
                                      3 * FOX_WIDTH + FOX_HEADS + MLA_Q_RANK,
                                      3 * FOX_WIDTH + FOX_HEADS + MLA_Q_RANK + MLA_KV_RANK,
                                      3 * FOX_WIDTH + FOX_HEADS + MLA_Q_RANK + MLA_KV_RANK + MLA_ROPE_DIM,
                                      3 * FOX_WIDTH + FOX_HEADS + MLA_Q_RANK + MLA_KV_RANK + MLA_ROPE_DIM
                                      + D_MODEL)
    for i in range(depth):
        w = w_in[i]
        zeros = lambda n: jnp.zeros((D_MODEL, n), w.dtype)
        w_misc = jnp.concatenate([w[:, o2:o3], zeros(KR_LANE - FOX_HEADS), w[:, o5:o6],
                                  zeros(LANES - KR_LANE - MLA_ROPE_DIM)], axis=1)
        w_big = jnp.concatenate([w[:, :o2], w[:, o3:o5], w[:, o6:], w_misc], axis=1).astype(BF16)
        q_f, k_f, v_f, cq, ckv, g, misc = _proj(h, w_big)

        bias_row = jnp.pad(fox_f_bias[i].astype(F32), (0, LANES - FOX_HEADS)).reshape(1, LANES)
        ccol, crow = _decay(misc, bias_row, batch, lp)
        o_fox = _attention(q_f, k_f, v_f, batch, lp, LANES, decay=(ccol, crow))

        qk_dim = MLA_NOPE_DIM + MLA_ROPE_DIM
        wuq = _pad_heads(mla_w_uq[i], MLA_HEADS, qk_dim, 0, qk_dim).astype(BF16)
        kv_dim = MLA_NOPE_DIM + MLA_V_DIM
        wuk = _pad_heads(mla_w_ukv[i], MLA_HEADS, kv_dim, 0, MLA_NOPE_DIM).astype(BF16)
        wuv = mla_w_ukv[i].reshape(MLA_KV_RANK, MLA_HEADS, kv_dim)[:, :, MLA_NOPE_DIM:]
        wuv = wuv.reshape(MLA_KV_RANK, MLA_WIDTH).astype(BF16)
        q_m, k_m, v_m = _mla_prep(cq, ckv, misc, cos_t, sin_t, row(mla_q_norm[i]),
                                  row(mla_kv_norm[i]), wuq, wuk, wuv, lp)
        o_mla = _attention(q_m, k_m, v_m, batch, lp, 2 * LANES)

        h1, comb_t = _post(o_fox, o_mla, g, h, fox_w_o[i].astype(BF16), mla_w_o[i].astype(BF16),
                           w_out[i].astype(BF16), row(ln1_g[i]), row(ln1_b[i]), rw_hi, rw_lo, rb,
                           alpha)

        wg = jnp.transpose(w_gate[i], (1, 0, 2)).reshape(D_MODEL, N_EXPERTS * D_EXPERT).astype(BF16)
        wu = jnp.transpose(w_up[i], (1, 0, 2)).reshape(D_MODEL, N_EXPERTS * D_EXPERT).astype(BF16)
        wd = w_down[i].reshape(N_EXPERTS * D_EXPERT, D_MODEL).astype(BF16)
        h = _moe(h1, comb_t, wg, wu, wd, row(ln2_g[i]), row(ln2_b[i]), alpha)

    return h.reshape(batch, lp, D_MODEL)[:, BLOCK:]
```

```python
import functools
import math

import jax
import jax.numpy as jnp
from jax import lax
from jax.experimental import pallas as pl
from jax.experimental.pallas import tpu as pltpu

F32 = jnp.float32
BF16 = jnp.bfloat16

D_MODEL = 1024
N_META = 16
BLOCK = 128
N_PAD = BLOCK - N_META
NEG_INF = -1e30
LOG2E = math.log2(math.e)

FOX_HEADS = 8
FOX_HEAD_DIM = 64
FOX_WIDTH = FOX_HEADS * FOX_HEAD_DIM

MLA_HEADS = 8
MLA_NOPE_DIM = 64
MLA_ROPE_DIM = 32
MLA_V_DIM = 64
MLA_Q_RANK = 384
MLA_KV_RANK = 256
MLA_WIDTH = MLA_HEADS * MLA_V_DIM
ROPE_THETA = 10000.0

N_EXPERTS = 16
N_GROUPS = 4
EXPERTS_PER_GROUP = N_EXPERTS // N_GROUPS
D_EXPERT = 256

LN_EPS = 1e-5
RMS_EPS = 1e-6

LANES = 128
N_HEADS = FOX_HEADS
HEAD_PAIRS = N_HEADS // 2
WIDE = N_HEADS * LANES
KR_LANE = 64

E_HI, E_MID, E_LO, E_ONE, E_PAD = 0, N_HEADS, 2 * N_HEADS, 3 * N_HEADS, 3 * N_HEADS + 1
X_CQ, X_ONE_K, X_BIAS = FOX_HEAD_DIM, FOX_HEAD_DIM + 3, FOX_HEAD_DIM + 6
M_BIAS = MLA_NOPE_DIM + MLA_ROPE_DIM

C_K = 0
C_CQ = WIDE
C_CKV = C_CQ + MLA_Q_RANK
C_G = C_CKV + MLA_KV_RANK
C_MISC = C_G + 2 * D_MODEL
PROJ_COLS = C_MISC + LANES

VMEM_LIMIT = 56 * 1024 * 1024

T_ATTN = 384
TM_POST = 256
TM_MOE = 512
MOE_CHUNKS = 4

NT_DIMS = (((1,), (1,)), ((), ()))


def _cparams(sem):
    return pltpu.CompilerParams(dimension_semantics=sem, vmem_limit_bytes=VMEM_LIMIT)


def _layer_norm(x, g, b):
    mu = jnp.mean(x, axis=-1, keepdims=True)
    xc = x - mu
    var = jnp.mean(xc * xc, axis=-1, keepdims=True)
    return xc * lax.rsqrt(var + LN_EPS) * g + b


def _rms_norm(x, g):
    ms = jnp.mean(x * x, axis=-1, keepdims=True)
    return x * lax.rsqrt(ms + RMS_EPS) * g


def _ln_in_kernel(x_ref, meta_ref, g_ref, b_ref, o_ref):
    i = pl.program_id(1)

    @pl.when(i == 0)
    def _():
        o_ref[0:N_PAD, :] = jnp.zeros((N_PAD, D_MODEL), F32)
        o_ref[N_PAD:, :] = _layer_norm(meta_ref[...], g_ref[...], b_ref[...])

    @pl.when(i > 0)
    def _():
        o_ref[...] = _layer_norm(x_ref[...], g_ref[...], b_ref[...])


def _ln_in(x2d, meta, g, b, batch, seq):
    nb_in = seq // BLOCK
    nb_out = nb_in + 1
    return pl.pallas_call(
        _ln_in_kernel,
        out_shape=jax.ShapeDtypeStruct((batch * nb_out * BLOCK, D_MODEL), F32),
        grid=(batch, nb_out),
        in_specs=[
            pl.BlockSpec((BLOCK, D_MODEL), lambda bb, i: (bb * nb_in + jnp.maximum(i - 1, 0), 0)),
            pl.BlockSpec((N_META, D_MODEL), lambda bb, i: (0, 0)),
            pl.BlockSpec((1, D_MODEL), lambda bb, i: (0, 0)),
            pl.BlockSpec((1, D_MODEL), lambda bb, i: (0, 0)),
        ],
        out_specs=pl.BlockSpec((BLOCK, D_MODEL), lambda bb, i: (bb * nb_out + i, 0)),
        compiler_params=_cparams(("parallel", "arbitrary")),
        name="ln_in",
    )(x2d, meta, g, b)


def _proj_kernel(x_ref, wqt_ref, wvt_ref, w_ref, pqt_ref, pk_ref, bias_ref,
                 qt_ref, k_ref, vt_ref, cq_ref, ckv_ref, g_ref, misc_ref, carry_ref, *, nb):
    t = x_ref.shape[0]
    j = pl.program_id(0) % nb
    x = x_ref[...].astype(BF16)

    def mm(lo, hi):
        return jnp.dot(x, w_ref[:, lo:hi], preferred_element_type=F32)

    misc = mm(C_MISC, PROJ_COLS)
    misc_ref[...] = misc
    cq_ref[...] = mm(C_CQ, C_CKV)
    ckv_ref[...] = mm(C_CKV, C_G)
    g_ref[:, :D_MODEL] = mm(C_G, C_G + D_MODEL)
    g_ref[:, D_MODEL:] = mm(C_G + D_MODEL, C_MISC)

    @pl.when(j == 0)
    def _():
        carry_ref[...] = jnp.zeros_like(carry_ref)

    z = misc + bias_ref[...]
    logf = jnp.minimum(z, 0.0) - jnp.log1p(jnp.exp(-jnp.abs(z)))
    row = lax.broadcasted_iota(jnp.int32, (t, LANES), 0)
    lane = lax.broadcasted_iota(jnp.int32, (t, LANES), 1)
    is_pad = row + j * t < N_PAD
    c = jnp.where(is_pad, 0.0, logf)
    shift = 1
    while shift < t:
        c = c + jnp.where(row >= shift, pltpu.roll(c, shift, 0), 0.0)
        shift *= 2
    c = c + carry_ref[...]
    carry_ref[...] = c[t - 1:t, :]
    c2 = c * LOG2E
    hi = c2.astype(BF16).astype(F32)
    r1 = c2 - hi
    mid = r1.astype(BF16).astype(F32)
    lo = (r1 - mid).astype(BF16).astype(F32)
    feat = jnp.where(lane < E_MID, hi,
           jnp.where(lane < E_LO, pltpu.roll(mid, E_MID, 1),
           jnp.where(lane < E_ONE, pltpu.roll(lo, E_LO, 1),
           jnp.where(lane == E_ONE, 1.0,
           jnp.where((lane == E_PAD) & is_pad, 1.0, 0.0)))))
    feat = feat.astype(BF16)
    extra_k = jnp.dot(feat, pk_ref[...], preferred_element_type=F32)
    extra_qt = lax.dot_general(pqt_ref[...], feat, NT_DIMS, preferred_element_type=F32)

    k_ref[0] = (mm(C_K, C_CQ) + extra_k).astype(BF16)
    qt = lax.dot_general(wqt_ref[...], x, NT_DIMS, preferred_element_type=F32)
    qt = qt * (FOX_HEAD_DIM ** -0.5 * LOG2E)
    for h in range(N_HEADS):
        qt_ref[0, h * LANES:h * LANES + FOX_HEAD_DIM, :] = (
            qt[h * FOX_HEAD_DIM:(h + 1) * FOX_HEAD_DIM, :].astype(BF16))
        qt_ref[0, h * LANES + FOX_HEAD_DIM:(h + 1) * LANES, :] = (
            extra_qt[h * LANES + FOX_HEAD_DIM:(h + 1) * LANES, :].astype(BF16))
    vt_ref[0] = lax.dot_general(wvt_ref[...], x, NT_DIMS, preferred_element_type=F32).astype(BF16)


def _proj(h, wqt, wvt, w_big, pqt, pk, bias_row, lp):
    rows = h.shape[0]
    t = T_ATTN
    nt = rows // t
    row = lambda i: (i, 0)
    full = lambda i: (0, 0)
    blk = lambda i: (i, 0, 0)
    return pl.pallas_call(
        functools.partial(_proj_kernel, nb=lp // t),
        out_shape=(
            jax.ShapeDtypeStruct((nt, WIDE, t), BF16),
            jax.ShapeDtypeStruct((nt, t, WIDE), BF16),
            jax.ShapeDtypeStruct((nt, FOX_WIDTH, t), BF16),
            jax.ShapeDtypeStruct((rows, MLA_Q_RANK), F32),
            jax.ShapeDtypeStruct((rows, MLA_KV_RANK), F32),
            jax.ShapeDtypeStruct((rows, 2 * D_MODEL), F32),
            jax.ShapeDtypeStruct((rows, LANES), F32),
        ),
        grid=(nt,),
        in_specs=[
            pl.BlockSpec((t, D_MODEL), row),
            pl.BlockSpec((FOX_WIDTH, D_MODEL), full),
            pl.BlockSpec((FOX_WIDTH, D_MODEL), full),
            pl.BlockSpec((D_MODEL, PROJ_COLS), full),
            pl.BlockSpec((WIDE, LANES), full),
            pl.BlockSpec((LANES, WIDE), full),
            pl.BlockSpec((1, LANES), full),
        ],
        out_specs=(
            pl.BlockSpec((1, WIDE, t), blk),
            pl.BlockSpec((1, t, WIDE), blk),
            pl.BlockSpec((1, FOX_WIDTH, t), blk),
            pl.BlockSpec((t, MLA_Q_RANK), row),
            pl.BlockSpec((t, MLA_KV_RANK), row),
            pl.BlockSpec((t, 2 * D_MODEL), row),
            pl.BlockSpec((t, LANES), row),
        ),
        scratch_shapes=[pltpu.VMEM((1, LANES), F32)],
        compiler_params=_cparams(("arbitrary",)),
        name="in_proj",
    )(h, wqt, wvt, w_big, pqt, pk, bias_row)


def _decay_placement():
    pk = [[0.0] * WIDE for _ in range(LANES)]
    pqt = [[0.0] * LANES for _ in range(WIDE)]
    for h in range(N_HEADS):
        base = h * LANES
        for s, e in enumerate((E_HI, E_MID, E_LO)):
            pqt[base + X_CQ + s][e + h] = 1.0
            pk[E_ONE][base + X_CQ + s] = 1.0
            pqt[base + X_ONE_K + s][E_ONE] = 1.0
            pk[e + h][base + X_ONE_K + s] = -1.0
        pqt[base + X_BIAS][E_ONE] = 1.0
        pk[E_PAD][base + X_BIAS] = NEG_INF
    return jnp.array(pqt, F32).astype(BF16), jnp.array(pk, F32).astype(BF16)


def _attn_kernel(qt_ref, k_ref, vt_ref, o_ref, st_ref, *, t):
    qi = pl.program_id(2)
    qt = qt_ref[0]
    causal = (lax.broadcasted_iota(jnp.int32, (t, t), 0)
              <= lax.broadcasted_iota(jnp.int32, (t, t), 1))

    def scores(kj, slot):
        k = k_ref[kj]
        for jj in range(2):
            st_ref[slot, jj] = jnp.dot(k[:, jj * LANES:(jj + 1) * LANES],
                                       qt[jj * LANES:(jj + 1) * LANES, :],
                                       preferred_element_type=F32)

    def consume(kj, slot, state, diagonal):
        vt = vt_ref[kj]
        out = []
        for jj in range(2):
            m, l, acc = state[jj]
            st = st_ref[slot, jj]
            if diagonal:
                st = jnp.where(causal, st, NEG_INF)
            m_new = jnp.maximum(m, jnp.max(st, axis=0, keepdims=True))
            alpha = jnp.exp2(m - m_new)
            p = jnp.exp2(st - m_new)
            l_new = alpha * l + jnp.sum(p, axis=0, keepdims=True)
            pv = jnp.dot(vt[jj * MLA_V_DIM:(jj + 1) * MLA_V_DIM, :], p.astype(BF16),
                         preferred_element_type=F32)
            out.append((m_new, l_new, alpha * acc + pv))
        return tuple(out)

    def pair(i, state):
        c0 = 2 * i
        scores(c0 + 1, 1)
        state = consume(c0, 0, state, False)
        scores(c0 + 2, 0)
        return consume(c0 + 1, 1, state, False)

    def odd_tail(state):
        scores(qi, 1)
        state = consume(qi - 1, 0, state, False)
        return consume(qi, 1, state, True)

    def even_tail(state):
        return consume(qi, 0, state, True)

    init_one = (jnp.full((1, t), NEG_INF, F32), jnp.zeros((1, t), F32),
                jnp.zeros((MLA_V_DIM, t), F32))
    scores(0, 0)
    state = lax.fori_loop(0, qi // 2, pair, (init_one, init_one))
    (_, l0, a0), (_, l1, a1) = lax.cond(qi % 2 == 1, odd_tail, even_tail, state)
    ot = jnp.concatenate([a0 / l0, a1 / l1], axis=0)
    o_ref[...] = ot.T.astype(o_ref.dtype)


def _attention(qt, k, vt, batch, lp, name):
    t = T_ATTN
    nq = lp // t
    return pl.pallas_call(
        functools.partial(_attn_kernel, t=t),
        out_shape=jax.ShapeDtypeStruct((batch * lp, HEAD_PAIRS * LANES), BF16),
        grid=(batch, HEAD_PAIRS, nq),
        in_specs=[
            pl.BlockSpec((1, 2 * LANES, t), lambda b, hp, qi: (b * nq + qi, hp, 0)),
            pl.BlockSpec((nq, t, 2 * LANES), lambda b, hp, qi: (b, 0, hp)),
            pl.BlockSpec((nq, LANES, t), lambda b, hp, qi: (b, hp, 0)),
        ],
        out_specs=pl.BlockSpec((t, LANES), lambda b, hp, qi: (b * nq + qi, hp)),
        scratch_shapes=[pltpu.VMEM((2, 2, t, t), F32)],
        compiler_params=_cparams(("parallel", "parallel", "arbitrary")),
        name=name,
    )(qt, k, vt)


def _mla_prep_kernel(cq_ref, ckv_ref, misc_ref, cos_ref, sin_ref, cost_ref, sint_ref, qn_ref, kvn_ref,
                     wuqt_ref, wuk_ref, wuvt_ref, qt_ref, k_ref, vt_ref, *, nb):
    t = cq_ref.shape[0]
    j = pl.program_id(0) % nb
    half = MLA_ROPE_DIM // 2

    qn = _rms_norm(cq_ref[...], qn_ref[...]).astype(BF16)
    qt = lax.dot_general(wuqt_ref[...], qn, NT_DIMS, preferred_element_type=F32)
    qt = qt * ((MLA_NOPE_DIM + MLA_ROPE_DIM) ** -0.5 * LOG2E)
    cost, sint = cost_ref[...], sint_ref[...]
    tail = jnp.where(lax.broadcasted_iota(jnp.int32, (LANES - M_BIAS, t), 0) == 0, 1.0, 0.0)
    for h in range(N_HEADS):
        base = h * LANES
        x1 = qt[base + MLA_NOPE_DIM:base + MLA_NOPE_DIM + half, :]
        x2 = qt[base + MLA_NOPE_DIM + half:base + M_BIAS, :]
        qt_ref[0, base:base + MLA_NOPE_DIM, :] = qt[base:base + MLA_NOPE_DIM, :].astype(BF16)
        qt_ref[0, base + MLA_NOPE_DIM:base + MLA_NOPE_DIM + half, :] = (x1 * cost - x2 * sint).astype(BF16)
        qt_ref[0, base + MLA_NOPE_DIM + half:base + M_BIAS, :] = (x2 * cost + x1 * sint).astype(BF16)
        qt_ref[0, base + M_BIAS:base + LANES, :] = tail.astype(BF16)

    kvn = _rms_norm(ckv_ref[...], kvn_ref[...]).astype(BF16)
    k_nope = jnp.dot(kvn, wuk_ref[...], preferred_element_type=F32)
    lane = lax.broadcasted_iota(jnp.int32, (t, LANES), 1)
    row = lax.broadcasted_iota(jnp.int32, (t, LANES), 0)
    kr = jnp.where((lane >= KR_LANE) & (lane < KR_LANE + MLA_ROPE_DIM), misc_ref[...], 0.0)
    from_hi = pltpu.roll(kr, LANES - half, 1)
    from_lo = pltpu.roll(kr, half, 1)
    swapped = jnp.where(lane < KR_LANE + half, -from_hi, from_lo)
    k_rot = kr * cos_ref[...] + swapped * sin_ref[...]
    k_rot = jnp.where((lane == M_BIAS) & (row + j * t < N_PAD), NEG_INF, k_rot)
    k_ref[0] = (k_nope + jnp.concatenate([k_rot] * N_HEADS, axis=1)).astype(BF16)
    vt_ref[0] = lax.dot_general(wuvt_ref[...], kvn, NT_DIMS, preferred_element_type=F32).astype(BF16)


def _mla_prep(cq, ckv, misc, tables, qn, kvn, wuqt, wuk, wuvt, lp):
    rows = cq.shape[0]
    t = T_ATTN
    nt = rows // t
    nb = lp // t
    cos_t, sin_t, cos_tt, sin_tt = tables
    row = lambda i: (i, 0)
    pos = lambda i: (i % nb, 0)
    post = lambda i: (0, i % nb)
    full = lambda i: (0, 0)
    blk = lambda i: (i, 0, 0)
    half = MLA_ROPE_DIM // 2
    return pl.pallas_call(
        functools.partial(_mla_prep_kernel, nb=nb),
        out_shape=(
            jax.ShapeDtypeStruct((nt, WIDE, t), BF16),
            jax.ShapeDtypeStruct((nt, t, WIDE), BF16),
            jax.ShapeDtypeStruct((nt, MLA_WIDTH, t), BF16),
        ),
        grid=(nt,),
        in_specs=[
            pl.BlockSpec((t, MLA_Q_RANK), row),
            pl.BlockSpec((t, MLA_KV_RANK), row),
            pl.BlockSpec((t, LANES), row),
            pl.BlockSpec((t, LANES), pos),
            pl.BlockSpec((t, LANES), pos),
            pl.BlockSpec((half, t), post),
            pl.BlockSpec((half, t), post),
            pl.BlockSpec((1, MLA_Q_RANK), full),
            pl.BlockSpec((1, MLA_KV_RANK), full),
            pl.BlockSpec((WIDE, MLA_Q_RANK), full),
            pl.BlockSpec((MLA_KV_RANK, WIDE), full),
            pl.BlockSpec((MLA_WIDTH, MLA_KV_RANK), full),
        ],
        out_specs=(
            pl.BlockSpec((1, WIDE, t), blk),
            pl.BlockSpec((1, t, WIDE), blk),
            pl.BlockSpec((1, MLA_WIDTH, t), blk),
        ),
        compiler_params=_cparams(("parallel",)),
        name="mla_prep",
    )(cq, ckv, misc, cos_t, sin_t, cos_tt, sin_tt, qn, kvn, wuqt, wuk, wuvt)


def _top2_sum(a, b, c, d):
    hi1, lo1 = jnp.maximum(a, b), jnp.minimum(a, b)
    hi2, lo2 = jnp.maximum(c, d), jnp.minimum(c, d)
    return jnp.maximum(hi1, hi2) + jnp.maximum(jnp.minimum(hi1, hi2), jnp.maximum(lo1, lo2))


def _route(logits_t, bias_col):
    scores = jax.nn.sigmoid(logits_t)
    biased = scores + bias_col
    b = [biased[e:e + 1, :] for e in range(N_EXPERTS)]
    s = [scores[e:e + 1, :] for e in range(N_EXPERTS)]
    gscore = [_top2_sum(*b[EXPERTS_PER_GROUP * g:EXPERTS_PER_GROUP * (g + 1)]) for g in range(N_GROUPS)]
    best = gscore[0]
    gidx = jnp.zeros_like(best, dtype=jnp.int32)
    for g in range(1, N_GROUPS):
        better = gscore[g] > best
        gidx = jnp.where(better, g, gidx)
        best = jnp.where(better, gscore[g], best)
    in_g = [gidx == g for g in range(N_GROUPS)]

    def pick(vals, j):
        out = vals[j]
        for g in range(1, N_GROUPS):
            out = jnp.where(in_g[g], vals[EXPERTS_PER_GROUP * g + j], out)
        return out

    vb = [pick(b, j) for j in range(EXPERTS_PER_GROUP)]
    vs = [pick(s, j) for j in range(EXPERTS_PER_GROUP)]
    chosen = []
    for j in range(EXPERTS_PER_GROUP):
        rank = jnp.zeros_like(gidx)
        for i in range(EXPERTS_PER_GROUP):
            if i == j:
                continue
            ahead = (vb[i] >= vb[j]) if i < j else (vb[i] > vb[j])
            rank = rank + jnp.where(ahead, 1, 0)
        chosen.append(rank < 2)
    total = sum(jnp.where(chosen[j], vs[j], 0.0) for j in range(EXPERTS_PER_GROUP))
    gates = [jnp.where(chosen[j], vs[j] / total, 0.0) for j in range(EXPERTS_PER_GROUP)]
    rows = []
    for g in range(N_GROUPS):
        for j in range(EXPERTS_PER_GROUP):
            rows.append(jnp.where(in_g[g], gates[j], 0.0))
    return jnp.concatenate(rows, axis=0)


def _post_kernel(of_ref, om_ref, g_ref, h_ref, wfo_ref, wmo_ref, wout_ref, lng_ref, lnb_ref,
                 rwh_ref, rwl_ref, rb_ref, h1_ref, comb_ref, *, alpha):
    y_fox = jnp.dot(of_ref[...], wfo_ref[...], preferred_element_type=F32)
    y_mla = jnp.dot(om_ref[...], wmo_ref[...], preferred_element_type=F32)
    merged = (jax.nn.sigmoid(g_ref[:, :D_MODEL]) * y_fox
              + jax.nn.sigmoid(g_ref[:, D_MODEL:]) * y_mla)
    mix = jnp.dot(merged.astype(BF16), wout_ref[...], preferred_element_type=F32)
    h1 = _layer_norm(alpha * h_ref[...] + mix, lng_ref[...], lnb_ref[...])
    h1_ref[...] = h1
    h_hi = h1.astype(BF16)
    h_lo = (h1 - h_hi.astype(F32)).astype(BF16)
    rwh, rwl = rwh_ref[...], rwl_ref[...]
    logits_t = (lax.dot_general(rwh, h_hi, NT_DIMS, preferred_element_type=F32)
                + lax.dot_general(rwl, h_hi, NT_DIMS, preferred_element_type=F32)
                + lax.dot_general(rwh, h_lo, NT_DIMS, preferred_element_type=F32))
    comb_ref[...] = _route(logits_t, rb_ref[...])


def _post(o_fox, o_mla, g, h, wfo, wmo, wout, lng, lnb, rwh, rwl, rb, alpha):
    rows = h.shape[0]
    tm = TM_POST
    row = lambda i: (i, 0)
    full = lambda i: (0, 0)
    return pl.pallas_call(
        functools.partial(_post_kernel, alpha=alpha),
        out_shape=(
            jax.ShapeDtypeStruct((rows, D_MODEL), F32),
            jax.ShapeDtypeStruct((N_EXPERTS, rows), F32),
        ),
        grid=(rows // tm,),
        in_specs=[
            pl.BlockSpec((tm, FOX_WIDTH), row),
            pl.BlockSpec((tm, MLA_WIDTH), row),
            pl.BlockSpec((tm, 2 * D_MODEL), row),
            pl.BlockSpec((tm, D_MODEL), row),
            pl.BlockSpec((FOX_WIDTH, D_MODEL), full),
            pl.BlockSpec((MLA_WIDTH, D_MODEL), full),
            pl.BlockSpec((D_MODEL, D_MODEL), full),
            pl.BlockSpec((1, D_MODEL), full),
            pl.BlockSpec((1, D_MODEL), full),
            pl.BlockSpec((N_EXPERTS, D_MODEL), full),
            pl.BlockSpec((N_EXPERTS, D_MODEL), full),
            pl.BlockSpec((N_EXPERTS, 1), full),
        ],
        out_specs=(
            pl.BlockSpec((tm, D_MODEL), row),
            pl.BlockSpec((N_EXPERTS, tm), lambda i: (0, i)),
        ),
        compiler_params=_cparams(("parallel",)),
        name="merge_ln1_router",
    )(o_fox, o_mla, g, h, wfo, wmo, wout, lng, lnb, rwh, rwl, rb)


def _moe_kernel(h_ref, comb_ref, wg_ref, wu_ref, wd_ref, lng_ref, lnb_ref, o_ref,
                xb_ref, acc_ref, cw_ref, *, alpha):
    c = pl.program_id(1)
    tm = h_ref.shape[0]

    @pl.when(c == 0)
    def _():
        xb_ref[...] = h_ref[...].astype(BF16)
        acc_ref[...] = jnp.zeros_like(acc_ref)
        padded = jnp.concatenate([comb_ref[...], jnp.zeros((LANES - N_EXPERTS, tm), F32)], axis=0)
        cw_ref[...] = padded.T

    x = xb_ref[...]
    gate = jnp.dot(x, wg_ref[...], preferred_element_type=F32)
    up = jnp.dot(x, wu_ref[...], preferred_element_type=F32)
    hid = gate * jax.nn.sigmoid(gate) * up
    cw = cw_ref[...]
    lane = lax.broadcasted_iota(jnp.int32, (tm, LANES), 1)
    pieces = []
    for e in range(EXPERTS_PER_GROUP):
        w_e = jnp.sum(jnp.where(lane == c * EXPERTS_PER_GROUP + e, cw, 0.0), axis=1, keepdims=True)
        pieces.append((hid[:, e * D_EXPERT:(e + 1) * D_EXPERT] * w_e).astype(BF16))
    hid_w = jnp.concatenate(pieces, axis=1)
    acc_ref[...] += jnp.dot(hid_w, wd_ref[...], preferred_element_type=F32)

    @pl.when(c == MOE_CHUNKS - 1)
    def _():
        o_ref[...] = _layer_norm(alpha * h_ref[...] + acc_ref[...], lng_ref[...], lnb_ref[...])


def _moe(h1, comb_t, wg, wu, wd, lng, lnb, alpha):
    rows = h1.shape[0]
    tm = TM_MOE
    chunk = EXPERTS_PER_GROUP * D_EXPERT
    return pl.pallas_call(
        functools.partial(_moe_kernel, alpha=alpha),
        out_shape=jax.ShapeDtypeStruct((rows, D_MODEL), F32),
        grid=(rows // tm, MOE_CHUNKS),
        in_specs=[
            pl.BlockSpec((tm, D_MODEL), lambda i, c: (i, 0)),
            pl.BlockSpec((N_EXPERTS, tm), lambda i, c: (0, i)),
            pl.BlockSpec((D_MODEL, chunk), lambda i, c: (0, c)),
            pl.BlockSpec((D_MODEL, chunk), lambda i, c: (0, c)),
            pl.BlockSpec((chunk, D_MODEL), lambda i, c: (c, 0)),
            pl.BlockSpec((1, D_MODEL), lambda i, c: (0, 0)),
            pl.BlockSpec((1, D_MODEL), lambda i, c: (0, 0)),
        ],
        out_specs=pl.BlockSpec((tm, D_MODEL), lambda i, c: (i, 0)),
        scratch_shapes=[
            pltpu.VMEM((tm, D_MODEL), BF16),
            pltpu.VMEM((tm, D_MODEL), F32),
            pltpu.VMEM((tm, LANES), F32),
        ],
        compiler_params=_cparams(("parallel", "arbitrary")),
        name="moe_ln2",
    )(h1, comb_t, wg, wu, wd, lng, lnb)


def _rope_tables(lp):
    half = MLA_ROPE_DIM // 2
    pos = jnp.maximum(jnp.arange(lp, dtype=jnp.int32) - N_PAD, 0).astype(F32)
    inv = ROPE_THETA ** (-jnp.arange(half, dtype=F32) / half)
    ang = pos[:, None] * inv[None, :]
    cos, sin = jnp.cos(ang), jnp.sin(ang)
    ones = jnp.ones((lp, KR_LANE), F32)
    tail = LANES - KR_LANE - MLA_ROPE_DIM
    cos_t = jnp.concatenate([ones, cos, cos, jnp.ones((lp, tail), F32)], axis=1)
    sin_t = jnp.concatenate([0 * ones, sin, sin, jnp.zeros((lp, tail), F32)], axis=1)
    return cos_t, sin_t, cos.T, sin.T


def _pad_heads(w, n_heads, per_head, keep_lo, keep_hi):
    k = w.shape[0]
    w = w.reshape(k, n_heads, per_head)[:, :, keep_lo:keep_hi]
    w = jnp.pad(w, ((0, 0), (0, 0), (0, LANES - (keep_hi - keep_lo))))
    return w.reshape(k, n_heads * LANES)


def kernel(x, meta_tokens, ln_in_g, ln_in_b, w_in, fox_f_bias, fox_w_o, mla_q_norm, mla_w_uq,
           mla_kv_norm, mla_w_ukv, mla_w_o, w_out, ln1_g, ln1_b, router_w, router_b,
           w_gate, w_up, w_down, ln2_g, ln2_b):
    batch, seq, _ = x.shape
    depth = w_in.shape[0]
    lp = seq + BLOCK
    assert seq % BLOCK == 0 and lp % T_ATTN == 0
    assert (batch * lp) % TM_POST == 0 and (batch * lp) % TM_MOE == 0
    alpha = (2 * depth) ** 0.25
    row = lambda a: a.reshape(1, -1).astype(F32)

    h = _ln_in(x.reshape(batch * seq, D_MODEL), meta_tokens.astype(F32), row(ln_in_g), row(ln_in_b),
               batch, seq)
    tables = _rope_tables(lp)
    pqt, pk = _decay_placement()
    rw_t = router_w.T.astype(F32)
    rw_hi = rw_t.astype(BF16)
    rw_lo = (rw_t - rw_hi.astype(F32)).astype(BF16)
    rb = router_b.reshape(N_EXPERTS, 1).astype(F32)

    o_q = FOX_WIDTH
    o_k = o_q + FOX_WIDTH
    o_v = o_k + FOX_WIDTH
    o_f = o_v + FOX_HEADS
    o_cq = o_f + MLA_Q_RANK
    o_ckv = o_cq + MLA_KV_RANK
    o_kr = o_ckv + MLA_ROPE_DIM
    for i in range(depth):
        w = w_in[i]
        zeros = lambda n: jnp.zeros((D_MODEL, n), w.dtype)
        w_misc = jnp.concatenate([w[:, o_v:o_f], zeros(KR_LANE - FOX_HEADS), w[:, o_ckv:o_kr],
                                  zeros(LANES - KR_LANE - MLA_ROPE_DIM)], axis=1)
        w_k = _pad_heads(w[:, o_q:o_k], FOX_HEADS, FOX_HEAD_DIM, 0, FOX_HEAD_DIM)
        w_big = jnp.concatenate([w_k, w[:, o_f:o_ckv], w[:, o_kr:], w_misc], axis=1).astype(BF16)
        wqt = w[:, :o_q].T.astype(BF16)
        wvt = w[:, o_k:o_v].T.astype(BF16)
        bias_row = jnp.pad(fox_f_bias[i].astype(F32), (0, LANES - FOX_HEADS)).reshape(1, LANES)
        qt_f, k_f, vt_f, cq, ckv, g, misc = _proj(h, wqt, wvt, w_big, pqt, pk, bias_row, lp)
        o_fox = _attention(qt_f, k_f, vt_f, batch, lp, "fox_attn")

        qk_dim = MLA_NOPE_DIM + MLA_ROPE_DIM
        wuqt = _pad_heads(mla_w_uq[i], MLA_HEADS, qk_dim, 0, qk_dim).T.astype(BF16)
        kv_dim = MLA_NOPE_DIM + MLA_V_DIM
        wuk = _pad_heads(mla_w_ukv[i], MLA_HEADS, kv_dim, 0, MLA_NOPE_DIM).astype(BF16)
        wuv = mla_w_ukv[i].reshape(MLA_KV_RANK, MLA_HEADS, kv_dim)[:, :, MLA_NOPE_DIM:]
        wuvt = wuv.reshape(MLA_KV_RANK, MLA_WIDTH).T.astype(BF16)
        qt_m, k_m, vt_m = _mla_prep(cq, ckv, misc, tables, row(mla_q_norm[i]), row(mla_kv_norm[i]),
                                    wuqt, wuk, wuvt, lp)
        o_mla = _attention(qt_m, k_m, vt_m, batch, lp, "mla_attn")

        h1, comb_t = _post(o_fox, o_mla, g, h, fox_w_o[i].astype(BF16), mla_w_o[i].astype(BF16),
                           w_out[i].astype(BF16), row(ln1_g[i]), row(ln1_b[i]), rw_hi, rw_lo, rb,
                           alpha)

        wg = jnp.transpose(w_gate[i], (1, 0, 2)).reshape(D_MODEL, N_EXPERTS * D_EXPERT).astype(BF16)
        wu = jnp.transpose(w_up[i], (1, 0, 2)).reshape(D_MODEL, N_EXPERTS * D_EXPERT).astype(BF16)
        wd = w_down[i].reshape(N_EXPERTS * D_EXPERT, D_MODEL).astype(BF16)
        h = _moe(h1, comb_t, wg, wu, wd, row(ln2_g[i]), row(ln2_b[i]), alpha)

    return h.reshape(batch, lp, D_MODEL)[:, BLOCK:]
```

```python
import functools
import math

import jax
import jax.numpy as jnp
from jax import lax
from jax.experimental import pallas as pl
from jax.experimental.pallas import tpu as pltpu

F32 = jnp.float32
BF16 = jnp.bfloat16

D_MODEL = 1024
N_META = 16
BLOCK = 128
NEG_INF = -1e30
LOG2E = math.log2(math.e)

FOX_HEADS = 8
FOX_HEAD_DIM = 64
FOX_WIDTH = FOX_HEADS * FOX_HEAD_DIM

MLA_HEADS = 8
MLA_NOPE_DIM = 64
MLA_ROPE_DIM = 32
MLA_V_DIM = 64
MLA_Q_RANK = 384
MLA_KV_RANK = 256
MLA_WIDTH = MLA_HEADS * MLA_V_DIM
ROPE_THETA = 10000.0

N_EXPERTS = 16
N_GROUPS = 4
EXPERTS_PER_GROUP = N_EXPERTS // N_GROUPS
D_EXPERT = 256

LN_EPS = 1e-5
RMS_EPS = 1e-6

LANES = 128
N_HEADS = FOX_HEADS
HEAD_PAIRS = N_HEADS // 2
WIDE = N_HEADS * LANES
KR_LANE = 64

E_HI, E_MID, E_LO, E_ONE, E_PAD = 0, N_HEADS, 2 * N_HEADS, 3 * N_HEADS, 3 * N_HEADS + 1
X_CQ, X_ONE_K, X_BIAS = FOX_HEAD_DIM, FOX_HEAD_DIM + 3, FOX_HEAD_DIM + 6
M_BIAS = MLA_NOPE_DIM + MLA_ROPE_DIM

C_K = 0
C_CQ = WIDE
C_CKV = C_CQ + MLA_Q_RANK
C_G = C_CKV + MLA_KV_RANK
C_MISC = C_G + 2 * D_MODEL
PROJ_COLS = C_MISC + LANES

VMEM_LIMIT = 56 * 1024 * 1024

T = 512
META_PER_TILE = T // BLOCK
TM_POST = 256
MOE_CHUNKS = 4

NT_DIMS = (((1,), (1,)), ((), ()))


def _cparams(sem):
    return pltpu.CompilerParams(dimension_semantics=sem, vmem_limit_bytes=VMEM_LIMIT)


def _layer_norm(x, g, b):
    mu = jnp.mean(x, axis=-1, keepdims=True)
    xc = x - mu
    var = jnp.mean(xc * xc, axis=-1, keepdims=True)
    return xc * lax.rsqrt(var + LN_EPS) * g + b


def _rms_norm(x, g):
    ms = jnp.mean(x * x, axis=-1, keepdims=True)
    return x * lax.rsqrt(ms + RMS_EPS) * g


def _ln_in_kernel(x_ref, meta_ref, g_ref, b_ref, o_ref, *, n_real):
    i = pl.program_id(0)

    @pl.when(i < n_real)
    def _():
        o_ref[...] = _layer_norm(x_ref[...], g_ref[...], b_ref[...])

    @pl.when(i >= n_real)
    def _():
        o_ref[...] = jnp.zeros_like(o_ref)
        m = _layer_norm(meta_ref[...], g_ref[...], b_ref[...])
        for jb in range(META_PER_TILE):
            o_ref[jb * BLOCK:jb * BLOCK + N_META, :] = m


def _ln_in(x2d, meta, g, b, n_real, n_tiles):
    full = lambda i: (0, 0)
    return pl.pallas_call(
        functools.partial(_ln_in_kernel, n_real=n_real),
        out_shape=jax.ShapeDtypeStruct((n_tiles * T, D_MODEL), F32),
        grid=(n_tiles,),
        in_specs=[
            pl.BlockSpec((T, D_MODEL), lambda i: (jnp.minimum(i, n_real - 1), 0)),
            pl.BlockSpec((N_META, D_MODEL), full),
            pl.BlockSpec((1, D_MODEL), full),
            pl.BlockSpec((1, D_MODEL), full),
        ],
        out_specs=pl.BlockSpec((T, D_MODEL), lambda i: (i, 0)),
        compiler_params=_cparams(("parallel",)),
        name="ln_in",
    )(x2d, meta, g, b)


def _proj_kernel(x_ref, wqt_ref, wvt_ref, w_ref, pqt_ref, pk_ref, bias_ref,
                 qt_ref, k_ref, vt_ref, cq_ref, ckv_ref, g_ref, misc_ref, carry_ref, c_ref,
                 *, n_real, tpb):
    i = pl.program_id(0)
    x = x_ref[...].astype(BF16)

    def mm(lo, hi):
        return jnp.dot(x, w_ref[:, lo:hi], preferred_element_type=F32)

    misc = mm(C_MISC, PROJ_COLS)
    misc_ref[...] = misc
    cq_ref[...] = mm(C_CQ, C_CKV)
    ckv_ref[...] = mm(C_CKV, C_G)
    g_ref[:, :D_MODEL] = mm(C_G, C_G + D_MODEL).astype(BF16)
    g_ref[:, D_MODEL:] = mm(C_G + D_MODEL, C_MISC).astype(BF16)

    z = misc + bias_ref[...]
    logf = jnp.minimum(z, 0.0) - jnp.log1p(jnp.exp(-jnp.abs(z)))
    row = lax.broadcasted_iota(jnp.int32, (T, LANES), 0)
    lane = lax.broadcasted_iota(jnp.int32, (T, LANES), 1)
    blk_row = row % BLOCK
    is_meta = i >= n_real

    @pl.when(jnp.logical_not(is_meta))
    def _():
        @pl.when(i % tpb == 0)
        def _():
            carry_ref[...] = jnp.zeros_like(carry_ref)

        c = logf
        shift = 1
        while shift < T:
            c = c + jnp.where(row >= shift, pltpu.roll(c, shift, 0), 0.0)
            shift *= 2
        c = c + carry_ref[...]
        carry_ref[...] = c[T - 1:T, :]
        c_ref[...] = c

    @pl.when(is_meta)
    def _():
        own = jnp.where(blk_row < N_META, logf, 0.0)
        s = own
        shift = 1
        while shift < BLOCK:
            s = s + jnp.where(blk_row < BLOCK - shift, pltpu.roll(s, T - shift, 0), 0.0)
            shift *= 2
        c_ref[...] = own - s

    is_pad = is_meta & (blk_row >= N_META)
    c2 = c_ref[...] * LOG2E
    hi = c2.astype(BF16).astype(F32)
    r1 = c2 - hi
    mid = r1.astype(BF16).astype(F32)
    lo = (r1 - mid).astype(BF16).astype(F32)
    feat = jnp.where(lane < E_MID, hi,
           jnp.where(lane < E_LO, pltpu.roll(mid, E_MID, 1),
           jnp.where(lane < E_ONE, pltpu.roll(lo, E_LO, 1),
           jnp.where(lane == E_ONE, 1.0,
           jnp.where((lane == E_PAD) & is_pad, 1.0, 0.0)))))
    feat = feat.astype(BF16)
    extra_k = jnp.dot(feat, pk_ref[...], preferred_element_type=F32)
    extra_qt = lax.dot_general(pqt_ref[...], feat, NT_DIMS, preferred_element_type=F32)

    k_ref[0] = (mm(C_K, C_CQ) + extra_k).astype(BF16)
    qt = lax.dot_general(wqt_ref[...], x, NT_DIMS, preferred_element_type=F32)
    qt = qt * (FOX_HEAD_DIM ** -0.5 * LOG2E)
    for h in range(N_HEADS):
        qt_ref[0, h * LANES:h * LANES + FOX_HEAD_DIM, :] = (
            qt[h * FOX_HEAD_DIM:(h + 1) * FOX_HEAD_DIM, :].astype(BF16))
        qt_ref[0, h * LANES + FOX_HEAD_DIM:(h + 1) * LANES, :] = (
            extra_qt[h * LANES + FOX_HEAD_DIM:(h + 1) * LANES, :].astype(BF16))
    vt_ref[0] = lax.dot_general(wvt_ref[...], x, NT_DIMS, preferred_element_type=F32).astype(BF16)


def _proj(h, wqt, wvt, w_big, pqt, pk, bias_row, n_real, tpb):
    rows = h.shape[0]
    nt = rows // T
    row = lambda i: (i, 0)
    full = lambda i: (0, 0)
    blk = lambda i: (i, 0, 0)
    return pl.pallas_call(
        functools.partial(_proj_kernel, n_real=n_real, tpb=tpb),
        out_shape=(
            jax.ShapeDtypeStruct((nt, WIDE, T), BF16),
            jax.ShapeDtypeStruct((nt, T, WIDE), BF16),
            jax.ShapeDtypeStruct((nt, FOX_WIDTH, T), BF16),
            jax.ShapeDtypeStruct((rows, MLA_Q_RANK), F32),
            jax.ShapeDtypeStruct((rows, MLA_KV_RANK), F32),
            jax.ShapeDtypeStruct((rows, 2 * D_MODEL), BF16),
            jax.ShapeDtypeStruct((rows, LANES), F32),
        ),
        grid=(nt,),
        in_specs=[
            pl.BlockSpec((T, D_MODEL), row),
            pl.BlockSpec((FOX_WIDTH, D_MODEL), full),
            pl.BlockSpec((FOX_WIDTH, D_MODEL), full),
            pl.BlockSpec((D_MODEL, PROJ_COLS), full),
            pl.BlockSpec((WIDE, LANES), full),
            pl.BlockSpec((LANES, WIDE), full),
            pl.BlockSpec((1, LANES), full),
        ],
        out_specs=(
            pl.BlockSpec((1, WIDE, T), blk),
            pl.BlockSpec((1, T, WIDE), blk),
            pl.BlockSpec((1, FOX_WIDTH, T), blk),
            pl.BlockSpec((T, MLA_Q_RANK), row),
            pl.BlockSpec((T, MLA_KV_RANK), row),
            pl.BlockSpec((T, 2 * D_MODEL), row),
            pl.BlockSpec((T, LANES), row),
        ),
        scratch_shapes=[pltpu.VMEM((1, LANES), F32), pltpu.VMEM((T, LANES), F32)],
        compiler_params=_cparams(("arbitrary",)),
        name="in_proj",
    )(h, wqt, wvt, w_big, pqt, pk, bias_row)


def _decay_placement():
    pk = [[0.0] * WIDE for _ in range(LANES)]
    pqt = [[0.0] * LANES for _ in range(WIDE)]
    for h in range(N_HEADS):
        base = h * LANES
        for s, e in enumerate((E_HI, E_MID, E_LO)):
            pqt[base + X_CQ + s][e + h] = 1.0
            pk[E_ONE][base + X_CQ + s] = 1.0
            pqt[base + X_ONE_K + s][E_ONE] = 1.0
            pk[e + h][base + X_ONE_K + s] = -1.0
        pqt[base + X_BIAS][E_ONE] = 1.0
        pk[E_PAD][base + X_BIAS] = NEG_INF
    return jnp.array(pqt, F32).astype(BF16), jnp.array(pk, F32).astype(BF16)


def _attn_kernel(qt_ref, k_ref, vt_ref, qtm_ref, km_ref, vtm_ref, o_ref, om_ref, st_ref):
    qi = pl.program_id(2)
    qt = qt_ref[0]
    km = km_ref[0]
    vtm = vtm_ref[0]

    def causal(n):
        return (lax.broadcasted_iota(jnp.int32, (n, n), 0)
                <= lax.broadcasted_iota(jnp.int32, (n, n), 1))

    def head(a, jj, width):
        return a[jj * width:(jj + 1) * width]

    @pl.when(qi == 0)
    def _():
        qtm = qtm_ref[0]
        outs = []
        for jj in range(2):
            st = jnp.dot(km[:, jj * LANES:(jj + 1) * LANES], head(qtm, jj, LANES),
                         preferred_element_type=F32)
            st = jnp.where(causal(BLOCK), st, NEG_INF)
            p = jnp.exp2(st - jnp.max(st, axis=0, keepdims=True))
            pv = jnp.dot(head(vtm, jj, MLA_V_DIM), p.astype(BF16), preferred_element_type=F32)
            outs.append(pv / jnp.sum(p, axis=0, keepdims=True))
        om_ref[...] = jnp.concatenate(outs, axis=0).T.astype(om_ref.dtype)

    state = []
    for jj in range(2):
        st = jnp.dot(km[:, jj * LANES:(jj + 1) * LANES], head(qt, jj, LANES),
                     preferred_element_type=F32)
        m = jnp.max(st, axis=0, keepdims=True)
        p = jnp.exp2(st - m)
        state.append((m, jnp.sum(p, axis=0, keepdims=True),
                      jnp.dot(head(vtm, jj, MLA_V_DIM), p.astype(BF16), preferred_element_type=F32)))
    state = tuple(state)

    def scores(kj, slot):
        k = k_ref[kj]
        for jj in range(2):
            st_ref[slot, jj] = jnp.dot(k[:, jj * LANES:(jj + 1) * LANES], head(qt, jj, LANES),
                                       preferred_element_type=F32)

    def consume(kj, slot, state, diagonal):
        vt = vt_ref[kj]
        out = []
        for jj in range(2):
            m, l, acc = state[jj]
            st = st_ref[slot, jj]
            if diagonal:
                st = jnp.where(causal(T), st, NEG_INF)
            m_new = jnp.maximum(m, jnp.max(st, axis=0, keepdims=True))
            alpha = jnp.exp2(m - m_new)
            p = jnp.exp2(st - m_new)
            l_new = alpha * l + jnp.sum(p, axis=0, keepdims=True)
            pv = jnp.dot(head(vt, jj, MLA_V_DIM), p.astype(BF16), preferred_element_type=F32)
            out.append((m_new, l_new, alpha * acc + pv))
        return tuple(out)

    def pair(i, state):
        c0 = 2 * i
        scores(c0 + 1, 1)
        state = consume(c0, 0, state, False)
        scores(c0 + 2, 0)
        return consume(c0 + 1, 1, state, False)

    def odd_tail(state):
        scores(qi, 1)
        state = consume(qi - 1, 0, state, False)
        return consume(qi, 1, state, True)

    def even_tail(state):
        return consume(qi, 0, state, True)

    scores(0, 0)
    state = lax.fori_loop(0, qi // 2, pair, state)
    (_, l0, a0), (_, l1, a1) = lax.cond(qi % 2 == 1, odd_tail, even_tail, state)
    ot = jnp.concatenate([a0 / l0, a1 / l1], axis=0)
    o_ref[...] = ot.T.astype(o_ref.dtype)


def _attention(qt, k, vt, batch, n_real, tpb, name):
    meta_tile = lambda b: n_real + b // META_PER_TILE
    meta_blk = lambda b: b % META_PER_TILE
    return pl.pallas_call(
        _attn_kernel,
        out_shape=(
            jax.ShapeDtypeStruct((n_real * T, HEAD_PAIRS * LANES), BF16),
            jax.ShapeDtypeStruct((batch * BLOCK, HEAD_PAIRS * LANES), BF16),
        ),
        grid=(batch, HEAD_PAIRS, tpb),
        in_specs=[
            pl.BlockSpec((1, 2 * LANES, T), lambda b, hp, qi: (b * tpb + qi, hp, 0)),
            pl.BlockSpec((tpb, T, 2 * LANES), lambda b, hp, qi: (b, 0, hp)),
            pl.BlockSpec((tpb, LANES, T), lambda b, hp, qi: (b, hp, 0)),
            pl.BlockSpec((1, 2 * LANES, BLOCK), lambda b, hp, qi: (meta_tile(b), hp, meta_blk(b))),
            pl.BlockSpec((1, BLOCK, 2 * LANES), lambda b, hp, qi: (meta_tile(b), meta_blk(b), hp)),
            pl.BlockSpec((1, LANES, BLOCK), lambda b, hp, qi: (meta_tile(b), hp, meta_blk(b))),
        ],
        out_specs=(
            pl.BlockSpec((T, LANES), lambda b, hp, qi: (b * tpb + qi, hp)),
            pl.BlockSpec((BLOCK, LANES), lambda b, hp, qi: (b, hp)),
        ),
        scratch_shapes=[pltpu.VMEM((2, 2, T, T), F32)],
        compiler_params=_cparams(("parallel", "parallel", "arbitrary")),
        name=name,
    )(qt, k, vt, qt, k, vt)


def _mla_prep_kernel(cq_ref, ckv_ref, misc_ref, cos_ref, sin_ref, cost_ref, sint_ref, qn_ref, kvn_ref,
                     wuqt_ref, wuk_ref, wuvt_ref, qt_ref, k_ref, vt_ref, *, n_real):
    i = pl.program_id(0)
    half = MLA_ROPE_DIM // 2

    qn = _rms_norm(cq_ref[...], qn_ref[...]).astype(BF16)
    qt = lax.dot_general(wuqt_ref[...], qn, NT_DIMS, preferred_element_type=F32)
    qt = qt * ((MLA_NOPE_DIM + MLA_ROPE_DIM) ** -0.5 * LOG2E)
    cost, sint = cost_ref[...], sint_ref[...]
    tail = jnp.where(lax.broadcasted_iota(jnp.int32, (LANES - M_BIAS, T), 0) == 0, 1.0, 0.0)
    for h in range(N_HEADS):
        base = h * LANES
        x1 = qt[base + MLA_NOPE_DIM:base + MLA_NOPE_DIM + half, :]
        x2 = qt[base + MLA_NOPE_DIM + half:base + M_BIAS, :]
        qt_ref[0, base:base + MLA_NOPE_DIM, :] = qt[base:base + MLA_NOPE_DIM, :].astype(BF16)
        qt_ref[0, base + MLA_NOPE_DIM:base + MLA_NOPE_DIM + half, :] = (x1 * cost - x2 * sint).astype(BF16)
        qt_ref[0, base + MLA_NOPE_DIM + half:base + M_BIAS, :] = (x2 * cost + x1 * sint).astype(BF16)
        qt_ref[0, base + M_BIAS:base + LANES, :] = tail.astype(BF16)

    kvn = _rms_norm(ckv_ref[...], kvn_ref[...]).astype(BF16)
    k_nope = jnp.dot(kvn, wuk_ref[...], preferred_element_type=F32)
    lane = lax.broadcasted_iota(jnp.int32, (T, LANES), 1)
    row = lax.broadcasted_iota(jnp.int32, (T, LANES), 0)
    is_pad = (i >= n_real) & (row % BLOCK >= N_META)
    kr = jnp.where((lane >= KR_LANE) & (lane < KR_LANE + MLA_ROPE_DIM), misc_ref[...], 0.0)
    from_hi = pltpu.roll(kr, LANES - half, 1)
    from_lo = pltpu.roll(kr, half, 1)
    swapped = jnp.where(lane < KR_LANE + half, -from_hi, from_lo)
    k_rot = kr * cos_ref[...] + swapped * sin_ref[...]
    k_rot = jnp.where((lane == M_BIAS) & is_pad, NEG_INF, k_rot)
    k_ref[0] = (k_nope + jnp.concatenate([k_rot] * N_HEADS, axis=1)).astype(BF16)
    vt_ref[0] = lax.dot_general(wuvt_ref[...], kvn, NT_DIMS, preferred_element_type=F32).astype(BF16)


def _mla_prep(cq, ckv, misc, tables, qn, kvn, wuqt, wuk, wuvt, n_real, tpb):
    rows = cq.shape[0]
    nt = rows // T
    cos_t, sin_t, cos_tt, sin_tt = tables
    row = lambda i: (i, 0)
    tab = lambda i: jnp.where(i < n_real, i % tpb, tpb)
    full = lambda i: (0, 0)
    blk = lambda i: (i, 0, 0)
    half = MLA_ROPE_DIM // 2
    return pl.pallas_call(
        functools.partial(_mla_prep_kernel, n_real=n_real),
        out_shape=(
            jax.ShapeDtypeStruct((nt, WIDE, T), BF16),
            jax.ShapeDtypeStruct((nt, T, WIDE), BF16),
            jax.ShapeDtypeStruct((nt, MLA_WIDTH, T), BF16),
        ),
        grid=(nt,),
        in_specs=[
            pl.BlockSpec((T, MLA_Q_RANK), row),
            pl.BlockSpec((T, MLA_KV_RANK), row),
            pl.BlockSpec((T, LANES), row),
            pl.BlockSpec((T, LANES), lambda i: (tab(i), 0)),
            pl.BlockSpec((T, LANES), lambda i: (tab(i), 0)),
            pl.BlockSpec((half, T), lambda i: (0, tab(i))),
            pl.BlockSpec((half, T), lambda i: (0, tab(i))),
            pl.BlockSpec((1, MLA_Q_RANK), full),
            pl.BlockSpec((1, MLA_KV_RANK), full),
            pl.BlockSpec((WIDE, MLA_Q_RANK), full),
            pl.BlockSpec((MLA_KV_RANK, WIDE), full),
            pl.BlockSpec((MLA_WIDTH, MLA_KV_RANK), full),
        ],
        out_specs=(
            pl.BlockSpec((1, WIDE, T), blk),
            pl.BlockSpec((1, T, WIDE), blk),
            pl.BlockSpec((1, MLA_WIDTH, T), blk),
        ),
        compiler_params=_cparams(("parallel",)),
        name="mla_prep",
    )(cq, ckv, misc, cos_t, sin_t, cos_tt, sin_tt, qn, kvn, wuqt, wuk, wuvt)


def _top2_sum(a, b, c, d):
    hi1, lo1 = jnp.maximum(a, b), jnp.minimum(a, b)
    hi2, lo2 = jnp.maximum(c, d), jnp.minimum(c, d)
    return jnp.maximum(hi1, hi2) + jnp.maximum(jnp.minimum(hi1, hi2), jnp.maximum(lo1, lo2))


def _route(logits_t, bias_col):
    scores = jax.nn.sigmoid(logits_t)
    biased = scores + bias_col
    b = [biased[e:e + 1, :] for e in range(N_EXPERTS)]
    s = [scores[e:e + 1, :] for e in range(N_EXPERTS)]
    gscore = [_top2_sum(*b[EXPERTS_PER_GROUP * g:EXPERTS_PER_GROUP * (g + 1)]) for g in range(N_GROUPS)]
    best = gscore[0]
    gidx = jnp.zeros_like(best, dtype=jnp.int32)
    for g in range(1, N_GROUPS):
        better = gscore[g] > best
        gidx = jnp.where(better, g, gidx)
        best = jnp.where(better, gscore[g], best)
    in_g = [gidx == g for g in range(N_GROUPS)]

    def pick(vals, j):
        out = vals[j]
        for g in range(1, N_GROUPS):
            out = jnp.where(in_g[g], vals[EXPERTS_PER_GROUP * g + j], out)
        return out

    vb = [pick(b, j) for j in range(EXPERTS_PER_GROUP)]
    vs = [pick(s, j) for j in range(EXPERTS_PER_GROUP)]
    chosen = []
    for j in range(EXPERTS_PER_GROUP):
        rank = jnp.zeros_like(gidx)
        for i in range(EXPERTS_PER_GROUP):
            if i == j:
                continue
            ahead = (vb[i] >= vb[j]) if i < j else (vb[i] > vb[j])
            rank = rank + jnp.where(ahead, 1, 0)
        chosen.append(rank < 2)
    total = sum(jnp.where(chosen[j], vs[j], 0.0) for j in range(EXPERTS_PER_GROUP))
    gates = [jnp.where(chosen[j], vs[j] / total, 0.0) for j in range(EXPERTS_PER_GROUP)]
    rows = []
    for g in range(N_GROUPS):
        for j in range(EXPERTS_PER_GROUP):
            rows.append(jnp.where(in_g[g], gates[j], 0.0))
    return jnp.concatenate(rows, axis=0)


def _post_kernel(of_ref, ofm_ref, om_ref, omm_ref, g_ref, h_ref, wfo_ref, wmo_ref, wout_ref,
                 lng_ref, lnb_ref, rwh_ref, rwl_ref, rb_ref, h1_ref, comb_ref, *, alpha, n_real_tiles):
    is_meta = pl.program_id(0) >= n_real_tiles
    o_fox = jnp.where(is_meta, ofm_ref[...], of_ref[...])
    o_mla = jnp.where(is_meta, omm_ref[...], om_ref[...])
    y_fox = jnp.dot(o_fox, wfo_ref[...], preferred_element_type=F32)
    y_mla = jnp.dot(o_mla, wmo_ref[...], preferred_element_type=F32)
    merged = (jax.nn.sigmoid(g_ref[:, :D_MODEL].astype(F32)) * y_fox
              + jax.nn.sigmoid(g_ref[:, D_MODEL:].astype(F32)) * y_mla)
    mix = jnp.dot(merged.astype(BF16), wout_ref[...], preferred_element_type=F32)
    h1 = _layer_norm(alpha * h_ref[...] + mix, lng_ref[...], lnb_ref[...])
    h1_ref[...] = h1
    h_hi = h1.astype(BF16)
    h_lo = (h1 - h_hi.astype(F32)).astype(BF16)
    rwh, rwl = rwh_ref[...], rwl_ref[...]
    logits_t = (lax.dot_general(rwh, h_hi, NT_DIMS, preferred_element_type=F32)
                + lax.dot_general(rwl, h_hi, NT_DIMS, preferred_element_type=F32)
                + lax.dot_general(rwh, h_lo, NT_DIMS, preferred_element_type=F32))
    comb_ref[...] = _route(logits_t, rb_ref[...])


def _post(o_fox, o_mla, g, h, wfo, wmo, wout, lng, lnb, rwh, rwl, rb, alpha):
    rows = h.shape[0]
    tm = TM_POST
    o_fox_r, o_fox_m = o_fox
    o_mla_r, o_mla_m = o_mla
    nrt = o_fox_r.shape[0] // tm
    row = lambda i: (i, 0)
    real = lambda i: (jnp.minimum(i, nrt - 1), 0)
    meta = lambda i: (jnp.maximum(i - nrt, 0), 0)
    full = lambda i: (0, 0)
    return pl.pallas_call(
        functools.partial(_post_kernel, alpha=alpha, n_real_tiles=nrt),
        out_shape=(
            jax.ShapeDtypeStruct((rows, D_MODEL), F32),
            jax.ShapeDtypeStruct((N_EXPERTS, rows), F32),
        ),
        grid=(rows // tm,),
        in_specs=[
            pl.BlockSpec((tm, FOX_WIDTH), real),
            pl.BlockSpec((tm, FOX_WIDTH), meta),
            pl.BlockSpec((tm, MLA_WIDTH), real),
            pl.BlockSpec((tm, MLA_WIDTH), meta),
            pl.BlockSpec((tm, 2 * D_MODEL), row),
            pl.BlockSpec((tm, D_MODEL), row),
            pl.BlockSpec((FOX_WIDTH, D_MODEL), full),
            pl.BlockSpec((MLA_WIDTH, D_MODEL), full),
            pl.BlockSpec((D_MODEL, D_MODEL), full),
            pl.BlockSpec((1, D_MODEL), full),
            pl.BlockSpec((1, D_MODEL), full),
            pl.BlockSpec((N_EXPERTS, D_MODEL), full),
            pl.BlockSpec((N_EXPERTS, D_MODEL), full),
            pl.BlockSpec((N_EXPERTS, 1), full),
        ],
        out_specs=(
            pl.BlockSpec((tm, D_MODEL), row),
            pl.BlockSpec((N_EXPERTS, tm), lambda i: (0, i)),
        ),
        compiler_params=_cparams(("parallel",)),
        name="merge_ln1_router",
    )(o_fox_r, o_fox_m, o_mla_r, o_mla_m, g, h, wfo, wmo, wout, lng, lnb, rwh, rwl, rb)


def _moe_kernel(h_ref, comb_ref, wg_ref, wu_ref, wd_ref, lng_ref, lnb_ref, o_ref,
                xb_ref, acc_ref, cw_ref, *, alpha):
    c = pl.program_id(1)
    tm = h_ref.shape[0]

    @pl.when(c == 0)
    def _():
        xb_ref[...] = h_ref[...].astype(BF16)
        acc_ref[...] = jnp.zeros_like(acc_ref)
        padded = jnp.concatenate([comb_ref[...], jnp.zeros((LANES - N_EXPERTS, tm), F32)], axis=0)
        cw_ref[...] = padded.T

    x = xb_ref[...]
    gate = jnp.dot(x, wg_ref[...], preferred_element_type=F32)
    up = jnp.dot(x, wu_ref[...], preferred_element_type=F32)
    hid = gate * jax.nn.sigmoid(gate) * up
    cw = cw_ref[...]
    lane = lax.broadcasted_iota(jnp.int32, (tm, LANES), 1)
    pieces = []
    for e in range(EXPERTS_PER_GROUP):
        w_e = jnp.sum(jnp.where(lane == c * EXPERTS_PER_GROUP + e, cw, 0.0), axis=1, keepdims=True)
        pieces.append((hid[:, e * D_EXPERT:(e + 1) * D_EXPERT] * w_e).astype(BF16))
    hid_w = jnp.concatenate(pieces, axis=1)
    acc_ref[...] += jnp.dot(hid_w, wd_ref[...], preferred_element_type=F32)

    @pl.when(c == MOE_CHUNKS - 1)
    def _():
        o_ref[...] = _layer_norm(alpha * h_ref[...] + acc_ref[...], lng_ref[...], lnb_ref[...])


def _moe(h1, comb_t, wg, wu, wd, lng, lnb, alpha, n_tiles):
    chunk = EXPERTS_PER_GROUP * D_EXPERT
    return pl.pallas_call(
        functools.partial(_moe_kernel, alpha=alpha),
        out_shape=jax.ShapeDtypeStruct((n_tiles * T, D_MODEL), F32),
        grid=(n_tiles, MOE_CHUNKS),
        in_specs=[
            pl.BlockSpec((T, D_MODEL), lambda i, c: (i, 0)),
            pl.BlockSpec((N_EXPERTS, T), lambda i, c: (0, i)),
            pl.BlockSpec((D_MODEL, chunk), lambda i, c: (0, c)),
            pl.BlockSpec((D_MODEL, chunk), lambda i, c: (0, c)),
            pl.BlockSpec((chunk, D_MODEL), lambda i, c: (c, 0)),
            pl.BlockSpec((1, D_MODEL), lambda i, c: (0, 0)),
            pl.BlockSpec((1, D_MODEL), lambda i, c: (0, 0)),
        ],
        out_specs=pl.BlockSpec((T, D_MODEL), lambda i, c: (i, 0)),
        scratch_shapes=[
            pltpu.VMEM((T, D_MODEL), BF16),
            pltpu.VMEM((T, D_MODEL), F32),
            pltpu.VMEM((T, LANES), F32),
        ],
        compiler_params=_cparams(("parallel", "arbitrary")),
        name="moe_ln2",
    )(h1, comb_t, wg, wu, wd, lng, lnb)


def _rope_tables(seq):
    half = MLA_ROPE_DIM // 2
    blk = jnp.arange(T, dtype=jnp.int32) % BLOCK
    pos = jnp.concatenate([jnp.arange(seq, dtype=jnp.int32) + N_META,
                           jnp.where(blk < N_META, blk, 0)]).astype(F32)
    rows = seq + T
    inv = ROPE_THETA ** (-jnp.arange(half, dtype=F32) / half)
    ang = pos[:, None] * inv[None, :]
    cos, sin = jnp.cos(ang), jnp.sin(ang)
    ones = jnp.ones((rows, KR_LANE), F32)
    tail = LANES - KR_LANE - MLA_ROPE_DIM
    cos_t = jnp.concatenate([ones, cos, cos, jnp.ones((rows, tail), F32)], axis=1)
    sin_t = jnp.concatenate([0 * ones, sin, sin, jnp.zeros((rows, tail), F32)], axis=1)
    return cos_t, sin_t, cos.T, sin.T


def _pad_heads(w, n_heads, per_head, keep_lo, keep_hi):
    k = w.shape[0]
    w = w.reshape(k, n_heads, per_head)[:, :, keep_lo:keep_hi]
    w = jnp.pad(w, ((0, 0), (0, 0), (0, LANES - (keep_hi - keep_lo))))
    return w.reshape(k, n_heads * LANES)


def kernel(x, meta_tokens, ln_in_g, ln_in_b, w_in, fox_f_bias, fox_w_o, mla_q_norm, mla_w_uq,
           mla_kv_norm, mla_w_ukv, mla_w_o, w_out, ln1_g, ln1_b, router_w, router_b,
           w_gate, w_up, w_down, ln2_g, ln2_b):
    batch, seq, _ = x.shape
    depth = w_in.shape[0]
    assert seq % T == 0 and batch % META_PER_TILE == 0
    tpb = seq // T
    n_real = batch * tpb
    n_tiles = n_real + batch // META_PER_TILE
    alpha = (2 * depth) ** 0.25
    row = lambda a: a.reshape(1, -1).astype(F32)

    h = _ln_in(x.reshape(batch * seq, D_MODEL), meta_tokens.astype(F32), row(ln_in_g), row(ln_in_b),
               n_real, n_tiles)
    tables = _rope_tables(seq)
    pqt, pk = _decay_placement()
    rw_t = router_w.T.astype(F32)
    rw_hi = rw_t.astype(BF16)
    rw_lo = (rw_t - rw_hi.astype(F32)).astype(BF16)
    rb = router_b.reshape(N_EXPERTS, 1).astype(F32)

    o_q = FOX_WIDTH
    o_k = o_q + FOX_WIDTH
    o_v = o_k + FOX_WIDTH
    o_f = o_v + FOX_HEADS
    o_cq = o_f + MLA_Q_RANK
    o_ckv = o_cq + MLA_KV_RANK
    o_kr = o_ckv + MLA_ROPE_DIM
    for i in range(depth):
        w = w_in[i]
        zeros = lambda n: jnp.zeros((D_MODEL, n), w.dtype)
        w_misc = jnp.concatenate([w[:, o_v:o_f], zeros(KR_LANE - FOX_HEADS), w[:, o_ckv:o_kr],
                                  zeros(LANES - KR_LANE - MLA_ROPE_DIM)], axis=1)
        w_k = _pad_heads(w[:, o_q:o_k], FOX_HEADS, FOX_HEAD_DIM, 0, FOX_HEAD_DIM)
        w_big = jnp.concatenate([w_k, w[:, o_f:o_ckv], w[:, o_kr:], w_misc], axis=1).astype(BF16)
        wqt = w[:, :o_q].T.astype(BF16)
        wvt = w[:, o_k:o_v].T.astype(BF16)
        bias_row = jnp.pad(fox_f_bias[i].astype(F32), (0, LANES - FOX_HEADS)).reshape(1, LANES)
        qt_f, k_f, vt_f, cq, ckv, g, misc = _proj(h, wqt, wvt, w_big, pqt, pk, bias_row, n_real, tpb)
        o_fox = _attention(qt_f, k_f, vt_f, batch, n_real, tpb, "fox_attn")

        qk_dim = MLA_NOPE_DIM + MLA_ROPE_DIM
        wuqt = _pad_heads(mla_w_uq[i], MLA_HEADS, qk_dim, 0, qk_dim).T.astype(BF16)
        kv_dim = MLA_NOPE_DIM + MLA_V_DIM
        wuk = _pad_heads(mla_w_ukv[i], MLA_HEADS, kv_dim, 0, MLA_NOPE_DIM).astype(BF16)
        wuv = mla_w_ukv[i].reshape(MLA_KV_RANK, MLA_HEADS, kv_dim)[:, :, MLA_NOPE_DIM:]
        wuvt = wuv.reshape(MLA_KV_RANK, MLA_WIDTH).T.astype(BF16)
        qt_m, k_m, vt_m = _mla_prep(cq, ckv, misc, tables, row(mla_q_norm[i]), row(mla_kv_norm[i]),
                                    wuqt, wuk, wuvt, n_real, tpb)
        o_mla = _attention(qt_m, k_m, vt_m, batch, n_real, tpb, "mla_attn")

        h1, comb_t = _post(o_fox, o_mla, g, h, fox_w_o[i].astype(BF16), mla_w_o[i].astype(BF16),
                           w_out[i].astype(BF16), row(ln1_g[i]), row(ln1_b[i]), rw_hi, rw_lo, rb,
                           alpha)

        wg = jnp.transpose(w_gate[i], (1, 0, 2)).reshape(D_MODEL, N_EXPERTS * D_EXPERT).astype(BF16)
        wu = jnp.transpose(w_up[i], (1, 0, 2)).reshape(D_MODEL, N_EXPERTS * D_EXPERT).astype(BF16)
        wd = w_down[i].reshape(N_EXPERTS * D_EXPERT, D_MODEL).astype(BF16)
        h = _moe(h1, comb_t, wg, wu, wd, row(ln2_g[i]), row(ln2_b[i]), alpha,
                 n_real if i == depth - 1 else n_tiles)

    return h.reshape(batch, seq, D_MODEL)
```

```python
import functools
import math

import jax
import jax.numpy as jnp
from jax import lax
from jax.experimental import pallas as pl
from jax.experimental.pallas import tpu as pltpu

F32 = jnp.float32
BF16 = jnp.bfloat16

D_MODEL = 1024
N_META = 16
BLOCK = 128
NEG_INF = -1e30
LOG2E = math.log2(math.e)

FOX_HEADS = 8
FOX_HEAD_DIM = 64
FOX_WIDTH = FOX_HEADS * FOX_HEAD_DIM

MLA_HEADS = 8
MLA_NOPE_DIM = 64
MLA_ROPE_DIM = 32
MLA_V_DIM = 64
MLA_Q_RANK = 384
MLA_KV_RANK = 256
MLA_WIDTH = MLA_HEADS * MLA_V_DIM
ROPE_THETA = 10000.0

N_EXPERTS = 16
N_GROUPS = 4
EXPERTS_PER_GROUP = N_EXPERTS // N_GROUPS
D_EXPERT = 256

LN_EPS = 1e-5
RMS_EPS = 1e-6

LANES = 128
N_HEADS = FOX_HEADS
ATTN_HEADS = 4
WIDE = N_HEADS * LANES
KR_LANE = 64

E_HI, E_MID, E_LO, E_ONE, E_PAD = 0, N_HEADS, 2 * N_HEADS, 3 * N_HEADS, 3 * N_HEADS + 1
X_CQ, X_ONE_K, X_BIAS = FOX_HEAD_DIM, FOX_HEAD_DIM + 3, FOX_HEAD_DIM + 6
M_BIAS = MLA_NOPE_DIM + MLA_ROPE_DIM

C_K = 0
C_CQ = WIDE
C_CKV = C_CQ + MLA_Q_RANK
C_G = C_CKV + MLA_KV_RANK
C_MISC = C_G + 2 * D_MODEL
PROJ_COLS = C_MISC + LANES

VMEM_LIMIT = 56 * 1024 * 1024

T = 512
META_PER_TILE = T // BLOCK
TM_POST = 512
MOE_CHUNKS = 4

NT_DIMS = (((1,), (1,)), ((), ()))


def _cparams(sem):
    return pltpu.CompilerParams(dimension_semantics=sem, vmem_limit_bytes=VMEM_LIMIT)


def _layer_norm(x, g, b):
    mu = jnp.mean(x, axis=-1, keepdims=True)
    xc = x - mu
    var = jnp.mean(xc * xc, axis=-1, keepdims=True)
    return xc * lax.rsqrt(var + LN_EPS) * g + b


def _rms_norm(x, g):
    ms = jnp.mean(x * x, axis=-1, keepdims=True)
    return x * lax.rsqrt(ms + RMS_EPS) * g


def _ln_in_kernel(x_ref, meta_ref, g_ref, b_ref, o_ref, *, n_real):
    i = pl.program_id(0)

    @pl.when(i < n_real)
    def _():
        o_ref[...] = _layer_norm(x_ref[...], g_ref[...], b_ref[...])

    @pl.when(i >= n_real)
    def _():
        o_ref[...] = jnp.zeros_like(o_ref)
        m = _layer_norm(meta_ref[...], g_ref[...], b_ref[...])
        for jb in range(META_PER_TILE):
            o_ref[jb * BLOCK:jb * BLOCK + N_META, :] = m


def _ln_in(x2d, meta, g, b, n_real, n_tiles):
    full = lambda i: (0, 0)
    return pl.pallas_call(
        functools.partial(_ln_in_kernel, n_real=n_real),
        out_shape=jax.ShapeDtypeStruct((n_tiles * T, D_MODEL), F32),
        grid=(n_tiles,),
        in_specs=[
            pl.BlockSpec((T, D_MODEL), lambda i: (jnp.minimum(i, n_real - 1), 0)),
            pl.BlockSpec((N_META, D_MODEL), full),
            pl.BlockSpec((1, D_MODEL), full),
            pl.BlockSpec((1, D_MODEL), full),
        ],
        out_specs=pl.BlockSpec((T, D_MODEL), lambda i: (i, 0)),
        compiler_params=_cparams(("parallel",)),
        name="ln_in",
    )(x2d, meta, g, b)


def _proj_kernel(x_ref, wqt_ref, wvt_ref, w_ref, pqt_ref, pk_ref, bias_ref,
                 qt_ref, k_ref, vt_ref, cq_ref, ckv_ref, g_ref, misc_ref, carry_ref, c_ref,
                 *, n_real, tpb):
    i = pl.program_id(0)
    x = x_ref[...].astype(BF16)

    def mm(lo, hi):
        return jnp.dot(x, w_ref[:, lo:hi], preferred_element_type=F32)

    misc = mm(C_MISC, PROJ_COLS)
    misc_ref[...] = misc
    cq_ref[...] = mm(C_CQ, C_CKV)
    ckv_ref[...] = mm(C_CKV, C_G)
    g_ref[:, :D_MODEL] = mm(C_G, C_G + D_MODEL).astype(BF16)
    g_ref[:, D_MODEL:] = mm(C_G + D_MODEL, C_MISC).astype(BF16)

    z = misc + bias_ref[...]
    logf = jnp.minimum(z, 0.0) - jnp.log1p(jnp.exp(-jnp.abs(z)))
    row = lax.broadcasted_iota(jnp.int32, (T, LANES), 0)
    lane = lax.broadcasted_iota(jnp.int32, (T, LANES), 1)
    blk_row = row % BLOCK
    is_meta = i >= n_real

    @pl.when(jnp.logical_not(is_meta))
    def _():
        @pl.when(i % tpb == 0)
        def _():
            carry_ref[...] = jnp.zeros_like(carry_ref)

        c = logf
        shift = 1
        while shift < T:
            c = c + jnp.where(row >= shift, pltpu.roll(c, shift, 0), 0.0)
            shift *= 2
        c = c + carry_ref[...]
        carry_ref[...] = c[T - 1:T, :]
        c_ref[...] = c

    @pl.when(is_meta)
    def _():
        own = jnp.where(blk_row < N_META, logf, 0.0)
        s = own
        shift = 1
        while shift < BLOCK:
            s = s + jnp.where(blk_row < BLOCK - shift, pltpu.roll(s, T - shift, 0), 0.0)
            shift *= 2
        c_ref[...] = own - s

    is_pad = is_meta & (blk_row >= N_META)
    c2 = c_ref[...] * LOG2E
    hi = c2.astype(BF16).astype(F32)
    r1 = c2 - hi
    mid = r1.astype(BF16).astype(F32)
    lo = (r1 - mid).astype(BF16).astype(F32)
    feat = jnp.where(lane < E_MID, hi,
           jnp.where(lane < E_LO, pltpu.roll(mid, E_MID, 1),
           jnp.where(lane < E_ONE, pltpu.roll(lo, E_LO, 1),
           jnp.where(lane == E_ONE, 1.0,
           jnp.where((lane == E_PAD) & is_pad, 1.0, 0.0)))))
    feat = feat.astype(BF16)
    extra_k = jnp.dot(feat, pk_ref[...], preferred_element_type=F32)
    extra_qt = lax.dot_general(pqt_ref[...], feat, NT_DIMS, preferred_element_type=F32)

    k_ref[0] = (mm(C_K, C_CQ) + extra_k).astype(BF16)
    qt = lax.dot_general(wqt_ref[...], x, NT_DIMS, preferred_element_type=F32)
    qt = qt * (FOX_HEAD_DIM ** -0.5 * LOG2E)
    for h in range(N_HEADS):
        qt_ref[0, h * LANES:h * LANES + FOX_HEAD_DIM, :] = (
            qt[h * FOX_HEAD_DIM:(h + 1) * FOX_HEAD_DIM, :].astype(BF16))
        qt_ref[0, h * LANES + FOX_HEAD_DIM:(h + 1) * LANES, :] = (
            extra_qt[h * LANES + FOX_HEAD_DIM:(h + 1) * LANES, :].astype(BF16))
    vt_ref[0] = lax.dot_general(wvt_ref[...], x, NT_DIMS, preferred_element_type=F32).astype(BF16)


def _proj(h, wqt, wvt, w_big, pqt, pk, bias_row, n_real, tpb):
    rows = h.shape[0]
    nt = rows // T
    row = lambda i: (i, 0)
    full = lambda i: (0, 0)
    blk = lambda i: (i, 0, 0)
    return pl.pallas_call(
        functools.partial(_proj_kernel, n_real=n_real, tpb=tpb),
        out_shape=(
            jax.ShapeDtypeStruct((nt, WIDE, T), BF16),
            jax.ShapeDtypeStruct((nt, T, WIDE), BF16),
            jax.ShapeDtypeStruct((nt, FOX_WIDTH, T), BF16),
            jax.ShapeDtypeStruct((rows, MLA_Q_RANK), F32),
            jax.ShapeDtypeStruct((rows, MLA_KV_RANK), F32),
            jax.ShapeDtypeStruct((rows, 2 * D_MODEL), BF16),
            jax.ShapeDtypeStruct((rows, LANES), F32),
        ),
        grid=(nt,),
        in_specs=[
            pl.BlockSpec((T, D_MODEL), row),
            pl.BlockSpec((FOX_WIDTH, D_MODEL), full),
            pl.BlockSpec((FOX_WIDTH, D_MODEL), full),
            pl.BlockSpec((D_MODEL, PROJ_COLS), full),
            pl.BlockSpec((WIDE, LANES), full),
            pl.BlockSpec((LANES, WIDE), full),
            pl.BlockSpec((1, LANES), full),
        ],
        out_specs=(
            pl.BlockSpec((1, WIDE, T), blk),
            pl.BlockSpec((1, T, WIDE), blk),
            pl.BlockSpec((1, FOX_WIDTH, T), blk),
            pl.BlockSpec((T, MLA_Q_RANK), row),
            pl.BlockSpec((T, MLA_KV_RANK), row),
            pl.BlockSpec((T, 2 * D_MODEL), row),
            pl.BlockSpec((T, LANES), row),
        ),
        scratch_shapes=[pltpu.VMEM((1, LANES), F32), pltpu.VMEM((T, LANES), F32)],
        compiler_params=_cparams(("arbitrary",)),
        name="in_proj",
    )(h, wqt, wvt, w_big, pqt, pk, bias_row)


def _decay_placement():
    pk = [[0.0] * WIDE for _ in range(LANES)]
    pqt = [[0.0] * LANES for _ in range(WIDE)]
    for h in range(N_HEADS):
        base = h * LANES
        for s, e in enumerate((E_HI, E_MID, E_LO)):
            pqt[base + X_CQ + s][e + h] = 1.0
            pk[E_ONE][base + X_CQ + s] = 1.0
            pqt[base + X_ONE_K + s][E_ONE] = 1.0
            pk[e + h][base + X_ONE_K + s] = -1.0
        pqt[base + X_BIAS][E_ONE] = 1.0
        pk[E_PAD][base + X_BIAS] = NEG_INF
    return jnp.array(pqt, F32).astype(BF16), jnp.array(pk, F32).astype(BF16)


def _attn_kernel(qt_ref, k_ref, vt_ref, qtm_ref, km_ref, vtm_ref, o_ref, om_ref, st_ref, stm_ref):
    qi = pl.program_id(2)
    nh = ATTN_HEADS
    qt = qt_ref[0]
    km = km_ref[0]
    vtm = vtm_ref[0]

    def causal(n):
        return (lax.broadcasted_iota(jnp.int32, (n, n), 0)
                <= lax.broadcasted_iota(jnp.int32, (n, n), 1))

    def head(a, jj, width):
        return a[jj * width:(jj + 1) * width]

    @pl.when(qi == 0)
    def _():
        qtm = qtm_ref[0]
        outs = []
        for jj in range(nh):
            st = jnp.dot(km[:, jj * LANES:(jj + 1) * LANES], head(qtm, jj, LANES),
                         preferred_element_type=F32)
            st = jnp.where(causal(BLOCK), st, NEG_INF)
            p = jnp.exp2(st - jnp.max(st, axis=0, keepdims=True))
            pv = jnp.dot(head(vtm, jj, MLA_V_DIM), p.astype(BF16), preferred_element_type=F32)
            outs.append(pv / jnp.sum(p, axis=0, keepdims=True))
        om_ref[...] = jnp.concatenate(outs, axis=0).T.astype(om_ref.dtype)

    def scores(kj, slot):
        k = k_ref[kj]
        for jj in range(nh):
            st_ref[slot, jj] = jnp.dot(k[:, jj * LANES:(jj + 1) * LANES], head(qt, jj, LANES),
                                       preferred_element_type=F32)

    def consume(kj, slot, state, diagonal):
        vt = vt_ref[kj]
        out = []
        for jj in range(nh):
            m, l, acc = state[jj]
            st = st_ref[slot, jj]
            m_new = m
            if diagonal:
                st = jnp.where(causal(T), st, NEG_INF)
                stm = stm_ref[jj]
                m_new = jnp.maximum(m_new, jnp.max(stm, axis=0, keepdims=True))
            m_new = jnp.maximum(m_new, jnp.max(st, axis=0, keepdims=True))
            alpha = jnp.exp2(m - m_new)
            p = jnp.exp2(st - m_new)
            l_new = alpha * l + jnp.sum(p, axis=0, keepdims=True)
            pv = jnp.dot(head(vt, jj, MLA_V_DIM), p.astype(BF16), preferred_element_type=F32)
            if diagonal:
                pm = jnp.exp2(stm - m_new)
                l_new = l_new + jnp.sum(pm, axis=0, keepdims=True)
                pv = pv + jnp.dot(head(vtm, jj, MLA_V_DIM), pm.astype(BF16),
                                  preferred_element_type=F32)
            out.append((m_new, l_new, alpha * acc + pv))
        return tuple(out)

    init_one = (jnp.full((1, T), NEG_INF, F32), jnp.zeros((1, T), F32),
                jnp.zeros((MLA_V_DIM, T), F32))
    state = (init_one,) * nh

    def pair(i, state):
        c0 = 2 * i
        scores(c0 + 1, 1)
        state = consume(c0, 0, state, False)
        scores(c0 + 2, 0)
        return consume(c0 + 1, 1, state, False)

    def odd_tail(state):
        scores(qi, 1)
        state = consume(qi - 1, 0, state, False)
        return consume(qi, 1, state, True)

    def even_tail(state):
        return consume(qi, 0, state, True)

    scores(0, 0)
    for jj in range(nh):
        stm_ref[jj] = jnp.dot(km[:, jj * LANES:(jj + 1) * LANES], head(qt, jj, LANES),
                              preferred_element_type=F32)
    state = lax.fori_loop(0, qi // 2, pair, state)
    state = lax.cond(qi % 2 == 1, odd_tail, even_tail, state)
    ot = jnp.concatenate([acc / l for (_, l, acc) in state], axis=0)
    o_ref[...] = ot.T.astype(o_ref.dtype)


def _attention(qt, k, vt, batch, n_real, tpb, name):
    nh = ATTN_HEADS
    qk_w, v_w = nh * LANES, nh * MLA_V_DIM
    meta_tile = lambda b: n_real + b // META_PER_TILE
    meta_blk = lambda b: b % META_PER_TILE
    return pl.pallas_call(
        _attn_kernel,
        out_shape=(
            jax.ShapeDtypeStruct((n_real * T, N_HEADS * MLA_V_DIM), BF16),
            jax.ShapeDtypeStruct((batch * BLOCK, N_HEADS * MLA_V_DIM), BF16),
        ),
        grid=(batch, N_HEADS // nh, tpb),
        in_specs=[
            pl.BlockSpec((1, qk_w, T), lambda b, hg, qi: (b * tpb + qi, hg, 0)),
            pl.BlockSpec((tpb, T, qk_w), lambda b, hg, qi: (b, 0, hg)),
            pl.BlockSpec((tpb, v_w, T), lambda b, hg, qi: (b, hg, 0)),
            pl.BlockSpec((1, qk_w, BLOCK), lambda b, hg, qi: (meta_tile(b), hg, meta_blk(b))),
            pl.BlockSpec((1, BLOCK, qk_w), lambda b, hg, qi: (meta_tile(b), meta_blk(b), hg)),
            pl.BlockSpec((1, v_w, BLOCK), lambda b, hg, qi: (meta_tile(b), hg, meta_blk(b))),
        ],
        out_specs=(
            pl.BlockSpec((T, v_w), lambda b, hg, qi: (b * tpb + qi, hg)),
            pl.BlockSpec((BLOCK, v_w), lambda b, hg, qi: (b, hg)),
        ),
        scratch_shapes=[pltpu.VMEM((2, nh, T, T), F32),
                        pltpu.VMEM((nh, BLOCK, T), F32)],
        compiler_params=_cparams(("parallel", "parallel", "arbitrary")),
        name=name,
    )(qt, k, vt, qt, k, vt)


def _mla_prep_kernel(cq_ref, ckv_ref, misc_ref, cos_ref, sin_ref, cost_ref, sint_ref, qn_ref, kvn_ref,
                     wuqt_ref, wuk_ref, wuvt_ref, qt_ref, k_ref, vt_ref, *, n_real):
    i = pl.program_id(0)
    half = MLA_ROPE_DIM // 2

    qn = _rms_norm(cq_ref[...], qn_ref[...]).astype(BF16)
    qt = lax.dot_general(wuqt_ref[...], qn, NT_DIMS, preferred_element_type=F32)
    qt = qt * ((MLA_NOPE_DIM + MLA_ROPE_DIM) ** -0.5 * LOG2E)
    cost, sint = cost_ref[...], sint_ref[...]
    tail = jnp.where(lax.broadcasted_iota(jnp.int32, (LANES - M_BIAS, T), 0) == 0, 1.0, 0.0)
    for h in range(N_HEADS):
        base = h * LANES
        x1 = qt[base + MLA_NOPE_DIM:base + MLA_NOPE_DIM + half, :]
        x2 = qt[base + MLA_NOPE_DIM + half:base + M_BIAS, :]
        qt_ref[0, base:base + MLA_NOPE_DIM, :] = qt[base:base + MLA_NOPE_DIM, :].astype(BF16)
        qt_ref[0, base + MLA_NOPE_DIM:base + MLA_NOPE_DIM + half, :] = (x1 * cost - x2 * sint).astype(BF16)
        qt_ref[0, base + MLA_NOPE_DIM + half:base + M_BIAS, :] = (x2 * cost + x1 * sint).astype(BF16)
        qt_ref[0, base + M_BIAS:base + LANES, :] = tail.astype(BF16)

    kvn = _rms_norm(ckv_ref[...], kvn_ref[...]).astype(BF16)
    k_nope = jnp.dot(kvn, wuk_ref[...], preferred_element_type=F32)
    lane = lax.broadcasted_iota(jnp.int32, (T, LANES), 1)
    row = lax.broadcasted_iota(jnp.int32, (T, LANES), 0)
    is_pad = (i >= n_real) & (row % BLOCK >= N_META)
    kr = jnp.where((lane >= KR_LANE) & (lane < KR_LANE + MLA_ROPE_DIM), misc_ref[...], 0.0)
    from_hi = pltpu.roll(kr, LANES - half, 1)
    from_lo = pltpu.roll(kr, half, 1)
    swapped = jnp.where(lane < KR_LANE + half, -from_hi, from_lo)
    k_rot = kr * cos_ref[...] + swapped * sin_ref[...]
    k_rot = jnp.where((lane == M_BIAS) & is_pad, NEG_INF, k_rot)
    k_ref[0] = (k_nope + jnp.concatenate([k_rot] * N_HEADS, axis=1)).astype(BF16)
    vt_ref[0] = lax.dot_general(wuvt_ref[...], kvn, NT_DIMS, preferred_element_type=F32).astype(BF16)


def _mla_prep(cq, ckv, misc, tables, qn, kvn, wuqt, wuk, wuvt, n_real, tpb):
    rows = cq.shape[0]
    nt = rows // T
    cos_t, sin_t, cos_tt, sin_tt = tables
    row = lambda i: (i, 0)
    tab = lambda i: jnp.where(i < n_real, i % tpb, tpb)
    full = lambda i: (0, 0)
    blk = lambda i: (i, 0, 0)
    half = MLA_ROPE_DIM // 2
    return pl.pallas_call(
        functools.partial(_mla_prep_kernel, n_real=n_real),
        out_shape=(
            jax.ShapeDtypeStruct((nt, WIDE, T), BF16),
            jax.ShapeDtypeStruct((nt, T, WIDE), BF16),
            jax.ShapeDtypeStruct((nt, MLA_WIDTH, T), BF16),
        ),
        grid=(nt,),
        in_specs=[
            pl.BlockSpec((T, MLA_Q_RANK), row),
            pl.BlockSpec((T, MLA_KV_RANK), row),
            pl.BlockSpec((T, LANES), row),
            pl.BlockSpec((T, LANES), lambda i: (tab(i), 0)),
            pl.BlockSpec((T, LANES), lambda i: (tab(i), 0)),
            pl.BlockSpec((half, T), lambda i: (0, tab(i))),
            pl.BlockSpec((half, T), lambda i: (0, tab(i))),
            pl.BlockSpec((1, MLA_Q_RANK), full),
            pl.BlockSpec((1, MLA_KV_RANK), full),
            pl.BlockSpec((WIDE, MLA_Q_RANK), full),
            pl.BlockSpec((MLA_KV_RANK, WIDE), full),
            pl.BlockSpec((MLA_WIDTH, MLA_KV_RANK), full),
        ],
        out_specs=(
            pl.BlockSpec((1, WIDE, T), blk),
            pl.BlockSpec((1, T, WIDE), blk),
            pl.BlockSpec((1, MLA_WIDTH, T), blk),
        ),
        compiler_params=_cparams(("parallel",)),
        name="mla_prep",
    )(cq, ckv, misc, cos_t, sin_t, cos_tt, sin_tt, qn, kvn, wuqt, wuk, wuvt)


def _top2_sum(a, b, c, d):
    hi1, lo1 = jnp.maximum(a, b), jnp.minimum(a, b)
    hi2, lo2 = jnp.maximum(c, d), jnp.minimum(c, d)
    return jnp.maximum(hi1, hi2) + jnp.maximum(jnp.minimum(hi1, hi2), jnp.maximum(lo1, lo2))


def _route(logits_t, bias_col):
    scores = jax.nn.sigmoid(logits_t)
    biased = scores + bias_col
    b = [biased[e:e + 1, :] for e in range(N_EXPERTS)]
    s = [scores[e:e + 1, :] for e in range(N_EXPERTS)]
    gscore = [_top2_sum(*b[EXPERTS_PER_GROUP * g:EXPERTS_PER_GROUP * (g + 1)]) for g in range(N_GROUPS)]
    best = gscore[0]
    gidx = jnp.zeros_like(best, dtype=jnp.int32)
    for g in range(1, N_GROUPS):
        better = gscore[g] > best
        gidx = jnp.where(better, g, gidx)
        best = jnp.where(better, gscore[g], best)
    in_g = [gidx == g for g in range(N_GROUPS)]

    def pick(vals, j):
        out = vals[j]
        for g in range(1, N_GROUPS):
            out = jnp.where(in_g[g], vals[EXPERTS_PER_GROUP * g + j], out)
        return out

    vb = [pick(b, j) for j in range(EXPERTS_PER_GROUP)]
    vs = [pick(s, j) for j in range(EXPERTS_PER_GROUP)]
    chosen = []
    for j in range(EXPERTS_PER_GROUP):
        rank = jnp.zeros_like(gidx)
        for i in range(EXPERTS_PER_GROUP):
            if i == j:
                continue
            ahead = (vb[i] >= vb[j]) if i < j else (vb[i] > vb[j])
            rank = rank + jnp.where(ahead, 1, 0)
        chosen.append(rank < 2)
    total = sum(jnp.where(chosen[j], vs[j], 0.0) for j in range(EXPERTS_PER_GROUP))
    gates = [jnp.where(chosen[j], vs[j] / total, 0.0) for j in range(EXPERTS_PER_GROUP)]
    rows = []
    for g in range(N_GROUPS):
        for j in range(EXPERTS_PER_GROUP):
            rows.append(jnp.where(in_g[g], gates[j], 0.0))
    return jnp.concatenate(rows, axis=0)


def _post_kernel(of_ref, ofm_ref, om_ref, omm_ref, g_ref, h_ref, wfo_ref, wmo_ref, wout_ref,
                 lng_ref, lnb_ref, rwh_ref, rwl_ref, rb_ref, h1_ref, comb_ref, *, alpha, n_real_tiles):
    is_meta = pl.program_id(0) >= n_real_tiles
    o_fox = jnp.where(is_meta, ofm_ref[...], of_ref[...])
    o_mla = jnp.where(is_meta, omm_ref[...], om_ref[...])
    y_fox = jnp.dot(o_fox, wfo_ref[...], preferred_element_type=F32)
    y_mla = jnp.dot(o_mla, wmo_ref[...], preferred_element_type=F32)
    merged = (jax.nn.sigmoid(g_ref[:, :D_MODEL].astype(F32)) * y_fox
              + jax.nn.sigmoid(g_ref[:, D_MODEL:].astype(F32)) * y_mla)
    mix = jnp.dot(merged.astype(BF16), wout_ref[...], preferred_element_type=F32)
    h1 = _layer_norm(alpha * h_ref[...] + mix, lng_ref[...], lnb_ref[...])
    h1_ref[...] = h1
    h_hi = h1.astype(BF16)
    h_lo = (h1 - h_hi.astype(F32)).astype(BF16)
    rwh, rwl = rwh_ref[...], rwl_ref[...]
    logits_t = (lax.dot_general(rwh, h_hi, NT_DIMS, preferred_element_type=F32)
                + lax.dot_general(rwl, h_hi, NT_DIMS, preferred_element_type=F32)
                + lax.dot_general(rwh, h_lo, NT_DIMS, preferred_element_type=F32))
    comb_ref[...] = _route(logits_t, rb_ref[...])


def _post(o_fox, o_mla, g, h, wfo, wmo, wout, lng, lnb, rwh, rwl, rb, alpha):
    rows = h.shape[0]
    tm = TM_POST
    o_fox_r, o_fox_m = o_fox
    o_mla_r, o_mla_m = o_mla
    nrt = o_fox_r.shape[0] // tm
    row = lambda i: (i, 0)
    real = lambda i: (jnp.minimum(i, nrt - 1), 0)
    meta = lambda i: (jnp.maximum(i - nrt, 0), 0)
    full = lambda i: (0, 0)
    return pl.pallas_call(
        functools.partial(_post_kernel, alpha=alpha, n_real_tiles=nrt),
        out_shape=(
            jax.ShapeDtypeStruct((rows, D_MODEL), F32),
            jax.ShapeDtypeStruct((N_EXPERTS, rows), F32),
        ),
        grid=(rows // tm,),
        in_specs=[
            pl.BlockSpec((tm, FOX_WIDTH), real),
            pl.BlockSpec((tm, FOX_WIDTH), meta),
            pl.BlockSpec((tm, MLA_WIDTH), real),
            pl.BlockSpec((tm, MLA_WIDTH), meta),
            pl.BlockSpec((tm, 2 * D_MODEL), row),
            pl.BlockSpec((tm, D_MODEL), row),
            pl.BlockSpec((FOX_WIDTH, D_MODEL), full),
            pl.BlockSpec((MLA_WIDTH, D_MODEL), full),
            pl.BlockSpec((D_MODEL, D_MODEL), full),
            pl.BlockSpec((1, D_MODEL), full),
            pl.BlockSpec((1, D_MODEL), full),
            pl.BlockSpec((N_EXPERTS, D_MODEL), full),
            pl.BlockSpec((N_EXPERTS, D_MODEL), full),
            pl.BlockSpec((N_EXPERTS, 1), full),
        ],
        out_specs=(
            pl.BlockSpec((tm, D_MODEL), row),
            pl.BlockSpec((N_EXPERTS, tm), lambda i: (0, i)),
        ),
        compiler_params=_cparams(("parallel",)),
        name="merge_ln1_router",
    )(o_fox_r, o_fox_m, o_mla_r, o_mla_m, g, h, wfo, wmo, wout, lng, lnb, rwh, rwl, rb)


def _moe_kernel(h_ref, comb_ref, wg_ref, wu_ref, wd_ref, lng_ref, lnb_ref, o_ref,
                xb_ref, acc_ref, cw_ref, *, alpha):
    c = pl.program_id(1)
    tm = h_ref.shape[0]

    @pl.when(c == 0)
    def _():
        xb_ref[...] = h_ref[...].astype(BF16)
        acc_ref[...] = jnp.zeros_like(acc_ref)
        padded = jnp.concatenate([comb_ref[...], jnp.zeros((LANES - N_EXPERTS, tm), F32)], axis=0)
        cw_ref[...] = padded.T

    x = xb_ref[...]
    gate = jnp.dot(x, wg_ref[...], preferred_element_type=F32)
    up = jnp.dot(x, wu_ref[...], preferred_element_type=F32)
    hid = gate * jax.nn.sigmoid(gate) * up
    cw = cw_ref[...]
    lane = lax.broadcasted_iota(jnp.int32, (tm, LANES), 1)
    pieces = []
    for e in range(EXPERTS_PER_GROUP):
        w_e = jnp.sum(jnp.where(lane == c * EXPERTS_PER_GROUP + e, cw, 0.0), axis=1, keepdims=True)
        pieces.append((hid[:, e * D_EXPERT:(e + 1) * D_EXPERT] * w_e).astype(BF16))
    hid_w = jnp.concatenate(pieces, axis=1)
    acc_ref[...] += jnp.dot(hid_w, wd_ref[...], preferred_element_type=F32)

    @pl.when(c == MOE_CHUNKS - 1)
    def _():
        o_ref[...] = _layer_norm(alpha * h_ref[...] + acc_ref[...], lng_ref[...], lnb_ref[...])


def _moe(h1, comb_t, wg, wu, wd, lng, lnb, alpha, n_tiles):
    chunk = EXPERTS_PER_GROUP * D_EXPERT
    return pl.pallas_call(
        functools.partial(_moe_kernel, alpha=alpha),
        out_shape=jax.ShapeDtypeStruct((n_tiles * T, D_MODEL), F32),
        grid=(n_tiles, MOE_CHUNKS),
        in_specs=[
            pl.BlockSpec((T, D_MODEL), lambda i, c: (i, 0)),
            pl.BlockSpec((N_EXPERTS, T), lambda i, c: (0, i)),
            pl.BlockSpec((D_MODEL, chunk), lambda i, c: (0, c)),
            pl.BlockSpec((D_MODEL, chunk), lambda i, c: (0, c)),
            pl.BlockSpec((chunk, D_MODEL), lambda i, c: (c, 0)),
            pl.BlockSpec((1, D_MODEL), lambda i, c: (0, 0)),
            pl.BlockSpec((1, D_MODEL), lambda i, c: (0, 0)),
        ],
        out_specs=pl.BlockSpec((T, D_MODEL), lambda i, c: (i, 0)),
        scratch_shapes=[
            pltpu.VMEM((T, D_MODEL), BF16),
            pltpu.VMEM((T, D_MODEL), F32),
            pltpu.VMEM((T, LANES), F32),
        ],
        compiler_params=_cparams(("parallel", "arbitrary")),
        name="moe_ln2",
    )(h1, comb_t, wg, wu, wd, lng, lnb)


def _rope_tables(seq):
    half = MLA_ROPE_DIM // 2
    blk = jnp.arange(T, dtype=jnp.int32) % BLOCK
    pos = jnp.concatenate([jnp.arange(seq, dtype=jnp.int32) + N_META,
                           jnp.where(blk < N_META, blk, 0)]).astype(F32)
    rows = seq + T
    inv = ROPE_THETA ** (-jnp.arange(half, dtype=F32) / half)
    ang = pos[:, None] * inv[None, :]
    cos, sin = jnp.cos(ang), jnp.sin(ang)
    ones = jnp.ones((rows, KR_LANE), F32)
    tail = LANES - KR_LANE - MLA_ROPE_DIM
    cos_t = jnp.concatenate([ones, cos, cos, jnp.ones((rows, tail), F32)], axis=1)
    sin_t = jnp.concatenate([0 * ones, sin, sin, jnp.zeros((rows, tail), F32)], axis=1)
    return cos_t, sin_t, cos.T, sin.T


def _pad_heads(w, n_heads, per_head, keep_lo, keep_hi):
    k = w.shape[0]
    w = w.reshape(k, n_heads, per_head)[:, :, keep_lo:keep_hi]
    w = jnp.pad(w, ((0, 0), (0, 0), (0, LANES - (keep_hi - keep_lo))))
    return w.reshape(k, n_heads * LANES)


def kernel(x, meta_tokens, ln_in_g, ln_in_b, w_in, fox_f_bias, fox_w_o, mla_q_norm, mla_w_uq,
           mla_kv_norm, mla_w_ukv, mla_w_o, w_out, ln1_g, ln1_b, router_w, router_b,
           w_gate, w_up, w_down, ln2_g, ln2_b):
    batch, seq, _ = x.shape
    depth = w_in.shape[0]
    assert seq % T == 0 and batch % META_PER_TILE == 0
    tpb = seq // T
    n_real = batch * tpb
    n_tiles = n_real + batch // META_PER_TILE
    alpha = (2 * depth) ** 0.25
    row = lambda a: a.reshape(1, -1).astype(F32)

    h = _ln_in(x.reshape(batch * seq, D_MODEL), meta_tokens.astype(F32), row(ln_in_g), row(ln_in_b),
               n_real, n_tiles)
    tables = _rope_tables(seq)
    pqt, pk = _decay_placement()
    rw_t = router_w.T.astype(F32)
    rw_hi = rw_t.astype(BF16)
    rw_lo = (rw_t - rw_hi.astype(F32)).astype(BF16)
    rb = router_b.reshape(N_EXPERTS, 1).astype(F32)

    o_q = FOX_WIDTH
    o_k = o_q + FOX_WIDTH
    o_v = o_k + FOX_WIDTH
    o_f = o_v + FOX_HEADS
    o_cq = o_f + MLA_Q_RANK
    o_ckv = o_cq + MLA_KV_RANK
    o_kr = o_ckv + MLA_ROPE_DIM
    for i in range(depth):
        w = w_in[i]
        zeros = lambda n: jnp.zeros((D_MODEL, n), w.dtype)
        w_misc = jnp.concatenate([w[:, o_v:o_f], zeros(KR_LANE - FOX_HEADS), w[:, o_ckv:o_kr],
                                  zeros(LANES - KR_LANE - MLA_ROPE_DIM)], axis=1)
        w_k = _pad_heads(w[:, o_q:o_k], FOX_HEADS, FOX_HEAD_DIM, 0, FOX_HEAD_DIM)
        w_big = jnp.concatenate([w_k, w[:, o_f:o_ckv], w[:, o_kr:], w_misc], axis=1).astype(BF16)
        wqt = w[:, :o_q].T.astype(BF16)
        wvt = w[:, o_k:o_v].T.astype(BF16)
        bias_row = jnp.pad(fox_f_bias[i].astype(F32), (0, LANES - FOX_HEADS)).reshape(1, LANES)
        qt_f, k_f, vt_f, cq, ckv, g, misc = _proj(h, wqt, wvt, w_big, pqt, pk, bias_row, n_real, tpb)
        o_fox = _attention(qt_f, k_f, vt_f, batch, n_real, tpb, "fox_attn")

        qk_dim = MLA_NOPE_DIM + MLA_ROPE_DIM
        wuqt = _pad_heads(mla_w_uq[i], MLA_HEADS, qk_dim, 0, qk_dim).T.astype(BF16)
        kv_dim = MLA_NOPE_DIM + MLA_V_DIM
        wuk = _pad_heads(mla_w_ukv[i], MLA_HEADS, kv_dim, 0, MLA_NOPE_DIM).astype(BF16)
        wuv = mla_w_ukv[i].reshape(MLA_KV_RANK, MLA_HEADS, kv_dim)[:, :, MLA_NOPE_DIM:]
        wuvt = wuv.reshape(MLA_KV_RANK, MLA_WIDTH).T.astype(BF16)
        qt_m, k_m, vt_m = _mla_prep(cq, ckv, misc, tables, row(mla_q_norm[i]), row(mla_kv_norm[i]),
                                    wuqt, wuk, wuvt, n_real, tpb)
        o_mla = _attention(qt_m, k_m, vt_m, batch, n_real, tpb, "mla_attn")

        h1, comb_t = _post(o_fox, o_mla, g, h, fox_w_o[i].astype(BF16), mla_w_o[i].astype(BF16),
                           w_out[i].astype(BF16), row(ln1_g[i]), row(ln1_b[i]), rw_hi, rw_lo, rb,
                           alpha)

        wg = jnp.transpose(w_gate[i], (1, 0, 2)).reshape(D_MODEL, N_EXPERTS * D_EXPERT).astype(BF16)
        wu = jnp.transpose(w_up[i], (1, 0, 2)).reshape(D_MODEL, N_EXPERTS * D_EXPERT).astype(BF16)
        wd = w_down[i].reshape(N_EXPERTS * D_EXPERT, D_MODEL).astype(BF16)
        h = _moe(h1, comb_t, wg, wu, wd, row(ln2_g[i]), row(ln2_b[i]), alpha,
                 n_real if i == depth - 1 else n_tiles)

    return h.reshape(batch, seq, D_MODEL)
```

```python
import functools
import math

import jax
import jax.numpy as jnp
from jax import lax
from jax.experimental import pallas as pl
from jax.experimental.pallas import tpu as pltpu

F32 = jnp.float32
BF16 = jnp.bfloat16

D_MODEL = 1024
N_META = 16
BLOCK = 128
NEG_INF = -1e30
LOG2E = math.log2(math.e)

FOX_HEADS = 8
FOX_HEAD_DIM = 64
FOX_WIDTH = FOX_HEADS * FOX_HEAD_DIM

MLA_HEADS = 8
MLA_NOPE_DIM = 64
MLA_ROPE_DIM = 32
MLA_V_DIM = 64
MLA_Q_RANK = 384
MLA_KV_RANK = 256
MLA_WIDTH = MLA_HEADS * MLA_V_DIM
ROPE_THETA = 10000.0

N_EXPERTS = 16
N_GROUPS = 4
EXPERTS_PER_GROUP = N_EXPERTS // N_GROUPS
D_EXPERT = 256

LN_EPS = 1e-5
RMS_EPS = 1e-6

LANES = 128
N_HEADS = FOX_HEADS
ATTN_HEADS = 4
WIDE = N_HEADS * LANES
KR_LANE = 64

E_HI, E_MID, E_LO, E_ONE, E_PAD = 0, N_HEADS, 2 * N_HEADS, 3 * N_HEADS, 3 * N_HEADS + 1
X_CQ, X_ONE_K, X_BIAS = FOX_HEAD_DIM, FOX_HEAD_DIM + 3, FOX_HEAD_DIM + 6
M_BIAS = MLA_NOPE_DIM + MLA_ROPE_DIM

C_K = 0
C_CQ = WIDE
C_CKV = C_CQ + MLA_Q_RANK
C_G = C_CKV + MLA_KV_RANK
C_MISC = C_G + 2 * D_MODEL
PROJ_COLS = C_MISC + LANES

VMEM_LIMIT = 56 * 1024 * 1024

T = 512
META_PER_TILE = T // BLOCK
TM_POST = 512
TM_MOE = 1024
MOE_CHUNK_LOG2 = 7
MOE_CHUNK = 1 << MOE_CHUNK_LOG2

NT_DIMS = (((1,), (1,)), ((), ()))


def _cparams(sem):
    return pltpu.CompilerParams(dimension_semantics=sem, vmem_limit_bytes=VMEM_LIMIT)


def _layer_norm(x, g, b):
    mu = jnp.mean(x, axis=-1, keepdims=True)
    xc = x - mu
    var = jnp.mean(xc * xc, axis=-1, keepdims=True)
    return xc * lax.rsqrt(var + LN_EPS) * g + b


def _rms_norm(x, g):
    ms = jnp.mean(x * x, axis=-1, keepdims=True)
    return x * lax.rsqrt(ms + RMS_EPS) * g


def _ln_in_kernel(x_ref, meta_ref, g_ref, b_ref, o_ref, *, n_real):
    i = pl.program_id(0)

    @pl.when(i < n_real)
    def _():
        o_ref[...] = _layer_norm(x_ref[...], g_ref[...], b_ref[...])

    @pl.when(i >= n_real)
    def _():
        o_ref[...] = jnp.zeros_like(o_ref)
        m = _layer_norm(meta_ref[...], g_ref[...], b_ref[...])
        for jb in range(META_PER_TILE):
            o_ref[jb * BLOCK:jb * BLOCK + N_META, :] = m


def _ln_in(x2d, meta, g, b, n_real, n_tiles):
    full = lambda i: (0, 0)
    return pl.pallas_call(
        functools.partial(_ln_in_kernel, n_real=n_real),
        out_shape=jax.ShapeDtypeStruct((n_tiles * T, D_MODEL), F32),
        grid=(n_tiles,),
        in_specs=[
            pl.BlockSpec((T, D_MODEL), lambda i: (jnp.minimum(i, n_real - 1), 0)),
            pl.BlockSpec((N_META, D_MODEL), full),
            pl.BlockSpec((1, D_MODEL), full),
            pl.BlockSpec((1, D_MODEL), full),
        ],
        out_specs=pl.BlockSpec((T, D_MODEL), lambda i: (i, 0)),
        compiler_params=_cparams(("parallel",)),
        name="ln_in",
    )(x2d, meta, g, b)


def _proj_kernel(x_ref, wqt_ref, wvt_ref, w_ref, pqt_ref, pk_ref, bias_ref,
                 qt_ref, k_ref, vt_ref, cq_ref, ckv_ref, g_ref, misc_ref, carry_ref, c_ref,
                 *, n_real, tpb):
    i = pl.program_id(0)
    x = x_ref[...].astype(BF16)

    def mm(lo, hi):
        return jnp.dot(x, w_ref[:, lo:hi], preferred_element_type=F32)

    misc = mm(C_MISC, PROJ_COLS)
    misc_ref[...] = misc
    cq_ref[...] = mm(C_CQ, C_CKV)
    ckv_ref[...] = mm(C_CKV, C_G)
    g_ref[:, :D_MODEL] = mm(C_G, C_G + D_MODEL).astype(BF16)
    g_ref[:, D_MODEL:] = mm(C_G + D_MODEL, C_MISC).astype(BF16)

    z = misc + bias_ref[...]
    logf = jnp.minimum(z, 0.0) - jnp.log1p(jnp.exp(-jnp.abs(z)))
    row = lax.broadcasted_iota(jnp.int32, (T, LANES), 0)
    lane = lax.broadcasted_iota(jnp.int32, (T, LANES), 1)
    blk_row = row % BLOCK
    is_meta = i >= n_real

    @pl.when(jnp.logical_not(is_meta))
    def _():
        @pl.when(i % tpb == 0)
        def _():
            carry_ref[...] = jnp.zeros_like(carry_ref)

        c = logf
        shift = 1
        while shift < T:
            c = c + jnp.where(row >= shift, pltpu.roll(c, shift, 0), 0.0)
            shift *= 2
        c = c + carry_ref[...]
        carry_ref[...] = c[T - 1:T, :]
        c_ref[...] = c

    @pl.when(is_meta)
    def _():
        own = jnp.where(blk_row < N_META, logf, 0.0)
        s = own
        shift = 1
        while shift < BLOCK:
            s = s + jnp.where(blk_row < BLOCK - shift, pltpu.roll(s, T - shift, 0), 0.0)
            shift *= 2
        c_ref[...] = own - s

    is_pad = is_meta & (blk_row >= N_META)
    c2 = c_ref[...] * LOG2E
    hi = c2.astype(BF16).astype(F32)
    r1 = c2 - hi
    mid = r1.astype(BF16).astype(F32)
    lo = (r1 - mid).astype(BF16).astype(F32)
    feat = jnp.where(lane < E_MID, hi,
           jnp.where(lane < E_LO, pltpu.roll(mid, E_MID, 1),
           jnp.where(lane < E_ONE, pltpu.roll(lo, E_LO, 1),
           jnp.where(lane == E_ONE, 1.0,
           jnp.where((lane == E_PAD) & is_pad, 1.0, 0.0)))))
    feat = feat.astype(BF16)
    extra_k = jnp.dot(feat, pk_ref[...], preferred_element_type=F32)
    extra_qt = lax.dot_general(pqt_ref[...], feat, NT_DIMS, preferred_element_type=F32)

    k_ref[0] = (mm(C_K, C_CQ) + extra_k).astype(BF16)
    qt = lax.dot_general(wqt_ref[...], x, NT_DIMS, preferred_element_type=F32)
    qt = qt * (FOX_HEAD_DIM ** -0.5 * LOG2E)
    for h in range(N_HEADS):
        qt_ref[0, h * LANES:h * LANES + FOX_HEAD_DIM, :] = (
            qt[h * FOX_HEAD_DIM:(h + 1) * FOX_HEAD_DIM, :].astype(BF16))
        qt_ref[0, h * LANES + FOX_HEAD_DIM:(h + 1) * LANES, :] = (
            extra_qt[h * LANES + FOX_HEAD_DIM:(h + 1) * LANES, :].astype(BF16))
    vt_ref[0] = lax.dot_general(wvt_ref[...], x, NT_DIMS, preferred_element_type=F32).astype(BF16)


def _proj(h, wqt, wvt, w_big, pqt, pk, bias_row, n_real, tpb):
    rows = h.shape[0]
    nt = rows // T
    row = lambda i: (i, 0)
    full = lambda i: (0, 0)
    blk = lambda i: (i, 0, 0)
    return pl.pallas_call(
        functools.partial(_proj_kernel, n_real=n_real, tpb=tpb),
        out_shape=(
            jax.ShapeDtypeStruct((nt, WIDE, T), BF16),
            jax.ShapeDtypeStruct((nt, T, WIDE), BF16),
            jax.ShapeDtypeStruct((nt, FOX_WIDTH, T), BF16),
            jax.ShapeDtypeStruct((rows, MLA_Q_RANK), F32),
            jax.ShapeDtypeStruct((rows, MLA_KV_RANK), F32),
            jax.ShapeDtypeStruct((rows, 2 * D_MODEL), BF16),
            jax.ShapeDtypeStruct((rows, LANES), F32),
        ),
        grid=(nt,),
        in_specs=[
            pl.BlockSpec((T, D_MODEL), row),
            pl.BlockSpec((FOX_WIDTH, D_MODEL), full),
            pl.BlockSpec((FOX_WIDTH, D_MODEL), full),
            pl.BlockSpec((D_MODEL, PROJ_COLS), full),
            pl.BlockSpec((WIDE, LANES), full),
            pl.BlockSpec((LANES, WIDE), full),
            pl.BlockSpec((1, LANES), full),
        ],
        out_specs=(
            pl.BlockSpec((1, WIDE, T), blk),
            pl.BlockSpec((1, T, WIDE), blk),
            pl.BlockSpec((1, FOX_WIDTH, T), blk),
            pl.BlockSpec((T, MLA_Q_RANK), row),
            pl.BlockSpec((T, MLA_KV_RANK), row),
            pl.BlockSpec((T, 2 * D_MODEL), row),
            pl.BlockSpec((T, LANES), row),
        ),
        scratch_shapes=[pltpu.VMEM((1, LANES), F32), pltpu.VMEM((T, LANES), F32)],
        compiler_params=_cparams(("arbitrary",)),
        name="in_proj",
    )(h, wqt, wvt, w_big, pqt, pk, bias_row)


def _decay_placement():
    pk = [[0.0] * WIDE for _ in range(LANES)]
    pqt = [[0.0] * LANES for _ in range(WIDE)]
    for h in range(N_HEADS):
        base = h * LANES
        for s, e in enumerate((E_HI, E_MID, E_LO)):
            pqt[base + X_CQ + s][e + h] = 1.0
            pk[E_ONE][base + X_CQ + s] = 1.0
            pqt[base + X_ONE_K + s][E_ONE] = 1.0
            pk[e + h][base + X_ONE_K + s] = -1.0
        pqt[base + X_BIAS][E_ONE] = 1.0
        pk[E_PAD][base + X_BIAS] = NEG_INF
    return jnp.array(pqt, F32).astype(BF16), jnp.array(pk, F32).astype(BF16)


def _attn_kernel(qt_ref, k_ref, vt_ref, qtm_ref, km_ref, vtm_ref, o_ref, om_ref, st_ref, stm_ref):
    qi = pl.program_id(2)
    nh = ATTN_HEADS
    qt = qt_ref[0]
    km = km_ref[0]
    vtm = vtm_ref[0]

    def causal(n):
        return (lax.broadcasted_iota(jnp.int32, (n, n), 0)
                <= lax.broadcasted_iota(jnp.int32, (n, n), 1))

    def head(a, jj, width):
        return a[jj * width:(jj + 1) * width]

    @pl.when(qi == 0)
    def _():
        qtm = qtm_ref[0]
        outs = []
        for jj in range(nh):
            st = jnp.dot(km[:, jj * LANES:(jj + 1) * LANES], head(qtm, jj, LANES),
                         preferred_element_type=F32)
            st = jnp.where(causal(BLOCK), st, NEG_INF)
            p = jnp.exp2(st - jnp.max(st, axis=0, keepdims=True))
            pv = jnp.dot(head(vtm, jj, MLA_V_DIM), p.astype(BF16), preferred_element_type=F32)
            outs.append(pv / jnp.sum(p, axis=0, keepdims=True))
        om_ref[...] = jnp.concatenate(outs, axis=0).T.astype(om_ref.dtype)

    def scores(kj, slot):
        k = k_ref[kj]
        for jj in range(nh):
            st_ref[slot, jj] = jnp.dot(k[:, jj * LANES:(jj + 1) * LANES], head(qt, jj, LANES),
                                       preferred_element_type=F32)

    def consume(kj, slot, state, diagonal):
        vt = vt_ref[kj]
        out = []
        for jj in range(nh):
            m, l, acc = state[jj]
            st = st_ref[slot, jj]
            m_new = m
            if diagonal:
                st = jnp.where(causal(T), st, NEG_INF)
                stm = stm_ref[jj]
                m_new = jnp.maximum(m_new, jnp.max(stm, axis=0, keepdims=True))
            m_new = jnp.maximum(m_new, jnp.max(st, axis=0, keepdims=True))
            alpha = jnp.exp2(m - m_new)
            p = jnp.exp2(st - m_new)
            l_new = alpha * l + jnp.sum(p, axis=0, keepdims=True)
            pv = jnp.dot(head(vt, jj, MLA_V_DIM), p.astype(BF16), preferred_element_type=F32)
            if diagonal:
                pm = jnp.exp2(stm - m_new)
                l_new = l_new + jnp.sum(pm, axis=0, keepdims=True)
                pv = pv + jnp.dot(head(vtm, jj, MLA_V_DIM), pm.astype(BF16),
                                  preferred_element_type=F32)
            out.append((m_new, l_new, alpha * acc + pv))
        return tuple(out)

    init_one = (jnp.full((1, T), NEG_INF, F32), jnp.zeros((1, T), F32),
                jnp.zeros((MLA_V_DIM, T), F32))
    state = (init_one,) * nh

    def pair(i, state):
        c0 = 2 * i
        scores(c0 + 1, 1)
        state = consume(c0, 0, state, False)
        scores(c0 + 2, 0)
        return consume(c0 + 1, 1, state, False)

    def odd_tail(state):
        scores(qi, 1)
        state = consume(qi - 1, 0, state, False)
        return consume(qi, 1, state, True)

    def even_tail(state):
        return consume(qi, 0, state, True)

    scores(0, 0)
    for jj in range(nh):
        stm_ref[jj] = jnp.dot(km[:, jj * LANES:(jj + 1) * LANES], head(qt, jj, LANES),
                              preferred_element_type=F32)
    state = lax.fori_loop(0, qi // 2, pair, state)
    state = lax.cond(qi % 2 == 1, odd_tail, even_tail, state)
    ot = jnp.concatenate([acc / l for (_, l, acc) in state], axis=0)
    o_ref[...] = ot.T.astype(o_ref.dtype)


def _attention(qt, k, vt, batch, n_real, tpb, name):
    nh = ATTN_HEADS
    qk_w, v_w = nh * LANES, nh * MLA_V_DIM
    meta_tile = lambda b: n_real + b // META_PER_TILE
    meta_blk = lambda b: b % META_PER_TILE
    return pl.pallas_call(
        _attn_kernel,
        out_shape=(
            jax.ShapeDtypeStruct((n_real * T, N_HEADS * MLA_V_DIM), BF16),
            jax.ShapeDtypeStruct((batch * BLOCK, N_HEADS * MLA_V_DIM), BF16),
        ),
        grid=(batch, N_HEADS // nh, tpb),
        in_specs=[
            pl.BlockSpec((1, qk_w, T), lambda b, hg, qi: (b * tpb + qi, hg, 0)),
            pl.BlockSpec((tpb, T, qk_w), lambda b, hg, qi: (b, 0, hg)),
            pl.BlockSpec((tpb, v_w, T), lambda b, hg, qi: (b, hg, 0)),
            pl.BlockSpec((1, qk_w, BLOCK), lambda b, hg, qi: (meta_tile(b), hg, meta_blk(b))),
            pl.BlockSpec((1, BLOCK, qk_w), lambda b, hg, qi: (meta_tile(b), meta_blk(b), hg)),
            pl.BlockSpec((1, v_w, BLOCK), lambda b, hg, qi: (meta_tile(b), hg, meta_blk(b))),
        ],
        out_specs=(
            pl.BlockSpec((T, v_w), lambda b, hg, qi: (b * tpb + qi, hg)),
            pl.BlockSpec((BLOCK, v_w), lambda b, hg, qi: (b, hg)),
        ),
        scratch_shapes=[pltpu.VMEM((2, nh, T, T), F32),
                        pltpu.VMEM((nh, BLOCK, T), F32)],
        compiler_params=_cparams(("parallel", "parallel", "arbitrary")),
        name=name,
    )(qt, k, vt, qt, k, vt)


def _mla_prep_kernel(cq_ref, ckv_ref, misc_ref, cos_ref, sin_ref, cost_ref, sint_ref, qn_ref, kvn_ref,
                     wuqt_ref, wuk_ref, wuvt_ref, qt_ref, k_ref, vt_ref, *, n_real):
    i = pl.program_id(0)
    half = MLA_ROPE_DIM // 2

    qn = _rms_norm(cq_ref[...], qn_ref[...]).astype(BF16)
    qt = lax.dot_general(wuqt_ref[...], qn, NT_DIMS, preferred_element_type=F32)
    qt = qt * ((MLA_NOPE_DIM + MLA_ROPE_DIM) ** -0.5 * LOG2E)
    cost, sint = cost_ref[...], sint_ref[...]
    tail = jnp.where(lax.broadcasted_iota(jnp.int32, (LANES - M_BIAS, T), 0) == 0, 1.0, 0.0)
    for h in range(N_HEADS):
        base = h * LANES
        x1 = qt[base + MLA_NOPE_DIM:base + MLA_NOPE_DIM + half, :]
        x2 = qt[base + MLA_NOPE_DIM + half:base + M_BIAS, :]
        qt_ref[0, base:base + MLA_NOPE_DIM, :] = qt[base:base + MLA_NOPE_DIM, :].astype(BF16)
        qt_ref[0, base + MLA_NOPE_DIM:base + MLA_NOPE_DIM + half, :] = (x1 * cost - x2 * sint).astype(BF16)
        qt_ref[0, base + MLA_NOPE_DIM + half:base + M_BIAS, :] = (x2 * cost + x1 * sint).astype(BF16)
        qt_ref[0, base + M_BIAS:base + LANES, :] = tail.astype(BF16)

    kvn = _rms_norm(ckv_ref[...], kvn_ref[...]).astype(BF16)
    k_nope = jnp.dot(kvn, wuk_ref[...], preferred_element_type=F32)
    lane = lax.broadcasted_iota(jnp.int32, (T, LANES), 1)
    row = lax.broadcasted_iota(jnp.int32, (T, LANES), 0)
    is_pad = (i >= n_real) & (row % BLOCK >= N_META)
    kr = jnp.where((lane >= KR_LANE) & (lane < KR_LANE + MLA_ROPE_DIM), misc_ref[...], 0.0)
    from_hi = pltpu.roll(kr, LANES - half, 1)
    from_lo = pltpu.roll(kr, half, 1)
    swapped = jnp.where(lane < KR_LANE + half, -from_hi, from_lo)
    k_rot = kr * cos_ref[...] + swapped * sin_ref[...]
    k_rot = jnp.where((lane == M_BIAS) & is_pad, NEG_INF, k_rot)
    k_ref[0] = (k_nope + jnp.concatenate([k_rot] * N_HEADS, axis=1)).astype(BF16)
    vt_ref[0] = lax.dot_general(wuvt_ref[...], kvn, NT_DIMS, preferred_element_type=F32).astype(BF16)


def _mla_prep(cq, ckv, misc, tables, qn, kvn, wuqt, wuk, wuvt, n_real, tpb):
    rows = cq.shape[0]
    nt = rows // T
    cos_t, sin_t, cos_tt, sin_tt = tables
    row = lambda i: (i, 0)
    tab = lambda i: jnp.where(i < n_real, i % tpb, tpb)
    full = lambda i: (0, 0)
    blk = lambda i: (i, 0, 0)
    half = MLA_ROPE_DIM // 2
    return pl.pallas_call(
        functools.partial(_mla_prep_kernel, n_real=n_real),
        out_shape=(
            jax.ShapeDtypeStruct((nt, WIDE, T), BF16),
            jax.ShapeDtypeStruct((nt, T, WIDE), BF16),
            jax.ShapeDtypeStruct((nt, MLA_WIDTH, T), BF16),
        ),
        grid=(nt,),
        in_specs=[
            pl.BlockSpec((T, MLA_Q_RANK), row),
            pl.BlockSpec((T, MLA_KV_RANK), row),
            pl.BlockSpec((T, LANES), row),
            pl.BlockSpec((T, LANES), lambda i: (tab(i), 0)),
            pl.BlockSpec((T, LANES), lambda i: (tab(i), 0)),
            pl.BlockSpec((half, T), lambda i: (0, tab(i))),
            pl.BlockSpec((half, T), lambda i: (0, tab(i))),
            pl.BlockSpec((1, MLA_Q_RANK), full),
            pl.BlockSpec((1, MLA_KV_RANK), full),
            pl.BlockSpec((WIDE, MLA_Q_RANK), full),
            pl.BlockSpec((MLA_KV_RANK, WIDE), full),
            pl.BlockSpec((MLA_WIDTH, MLA_KV_RANK), full),
        ],
        out_specs=(
            pl.BlockSpec((1, WIDE, T), blk),
            pl.BlockSpec((1, T, WIDE), blk),
            pl.BlockSpec((1, MLA_WIDTH, T), blk),
        ),
        compiler_params=_cparams(("parallel",)),
        name="mla_prep",
    )(cq, ckv, misc, cos_t, sin_t, cos_tt, sin_tt, qn, kvn, wuqt, wuk, wuvt)


def _top2_sum(a, b, c, d):
    hi1, lo1 = jnp.maximum(a, b), jnp.minimum(a, b)
    hi2, lo2 = jnp.maximum(c, d), jnp.minimum(c, d)
    return jnp.maximum(hi1, hi2) + jnp.maximum(jnp.minimum(hi1, hi2), jnp.maximum(lo1, lo2))


def _route(logits_t, bias_col):
    scores = jax.nn.sigmoid(logits_t)
    biased = scores + bias_col
    b = [biased[e:e + 1, :] for e in range(N_EXPERTS)]
    s = [scores[e:e + 1, :] for e in range(N_EXPERTS)]
    gscore = [_top2_sum(*b[EXPERTS_PER_GROUP * g:EXPERTS_PER_GROUP * (g + 1)]) for g in range(N_GROUPS)]
    best = gscore[0]
    gidx = jnp.zeros_like(best, dtype=jnp.int32)
    for g in range(1, N_GROUPS):
        better = gscore[g] > best
        gidx = jnp.where(better, g, gidx)
        best = jnp.where(better, gscore[g], best)
    in_g = [gidx == g for g in range(N_GROUPS)]

    def pick(vals, j):
        out = vals[j]
        for g in range(1, N_GROUPS):
            out = jnp.where(in_g[g], vals[EXPERTS_PER_GROUP * g + j], out)
        return out

    vb = [pick(b, j) for j in range(EXPERTS_PER_GROUP)]
    vs = [pick(s, j) for j in range(EXPERTS_PER_GROUP)]
    chosen = []
    for j in range(EXPERTS_PER_GROUP):
        rank = jnp.zeros_like(gidx)
        for i in range(EXPERTS_PER_GROUP):
            if i == j:
                continue
            ahead = (vb[i] >= vb[j]) if i < j else (vb[i] > vb[j])
            rank = rank + jnp.where(ahead, 1, 0)
        chosen.append(rank < 2)
    total = sum(jnp.where(chosen[j], vs[j], 0.0) for j in range(EXPERTS_PER_GROUP))
    gates = [jnp.where(chosen[j], vs[j] / total, 0.0) for j in range(EXPERTS_PER_GROUP)]
    rows = []
    for g in range(N_GROUPS):
        for j in range(EXPERTS_PER_GROUP):
            rows.append(jnp.where(in_g[g], gates[j], 0.0))
    return jnp.concatenate(rows, axis=0), gidx


def _post_kernel(of_ref, ofm_ref, om_ref, omm_ref, g_ref, h_ref, wfo_ref, wmo_ref, wout_ref,
                 lng_ref, lnb_ref, rwh_ref, rwl_ref, rb_ref, h1_ref, comb_ref, gid_ref,
                 *, alpha, n_real_tiles):
    is_meta = pl.program_id(0) >= n_real_tiles
    o_fox = jnp.where(is_meta, ofm_ref[...], of_ref[...])
    o_mla = jnp.where(is_meta, omm_ref[...], om_ref[...])
    y_fox = jnp.dot(o_fox, wfo_ref[...], preferred_element_type=F32)
    y_mla = jnp.dot(o_mla, wmo_ref[...], preferred_element_type=F32)
    merged = (jax.nn.sigmoid(g_ref[:, :D_MODEL].astype(F32)) * y_fox
              + jax.nn.sigmoid(g_ref[:, D_MODEL:].astype(F32)) * y_mla)
    mix = jnp.dot(merged.astype(BF16), wout_ref[...], preferred_element_type=F32)
    h1 = _layer_norm(alpha * h_ref[...] + mix, lng_ref[...], lnb_ref[...])
    h1_ref[...] = h1
    h_hi = h1.astype(BF16)
    h_lo = (h1 - h_hi.astype(F32)).astype(BF16)
    rwh, rwl = rwh_ref[...], rwl_ref[...]
    logits_t = (lax.dot_general(rwh, h_hi, NT_DIMS, preferred_element_type=F32)
                + lax.dot_general(rwl, h_hi, NT_DIMS, preferred_element_type=F32)
                + lax.dot_general(rwh, h_lo, NT_DIMS, preferred_element_type=F32))
    comb_ref[...], gid_ref[...] = _route(logits_t, rb_ref[...])


def _post(o_fox, o_mla, g, h, wfo, wmo, wout, lng, lnb, rwh, rwl, rb, alpha):
    rows = h.shape[0]
    tm = TM_POST
    o_fox_r, o_fox_m = o_fox
    o_mla_r, o_mla_m = o_mla
    nrt = o_fox_r.shape[0] // tm
    row = lambda i: (i, 0)
    real = lambda i: (jnp.minimum(i, nrt - 1), 0)
    meta = lambda i: (jnp.maximum(i - nrt, 0), 0)
    full = lambda i: (0, 0)
    return pl.pallas_call(
        functools.partial(_post_kernel, alpha=alpha, n_real_tiles=nrt),
        out_shape=(
            jax.ShapeDtypeStruct((rows, D_MODEL), F32),
            jax.ShapeDtypeStruct((N_EXPERTS, rows), F32),
            jax.ShapeDtypeStruct((1, rows), jnp.int32),
        ),
        grid=(rows // tm,),
        in_specs=[
            pl.BlockSpec((tm, FOX_WIDTH), real),
            pl.BlockSpec((tm, FOX_WIDTH), meta),
            pl.BlockSpec((tm, MLA_WIDTH), real),
            pl.BlockSpec((tm, MLA_WIDTH), meta),
            pl.BlockSpec((tm, 2 * D_MODEL), row),
            pl.BlockSpec((tm, D_MODEL), row),
            pl.BlockSpec((FOX_WIDTH, D_MODEL), full),
            pl.BlockSpec((MLA_WIDTH, D_MODEL), full),
            pl.BlockSpec((D_MODEL, D_MODEL), full),
            pl.BlockSpec((1, D_MODEL), full),
            pl.BlockSpec((1, D_MODEL), full),
            pl.BlockSpec((N_EXPERTS, D_MODEL), full),
            pl.BlockSpec((N_EXPERTS, D_MODEL), full),
            pl.BlockSpec((N_EXPERTS, 1), full),
        ],
        out_specs=(
            pl.BlockSpec((tm, D_MODEL), row),
            pl.BlockSpec((N_EXPERTS, tm), lambda i: (0, i)),
            pl.BlockSpec((1, tm), lambda i: (0, i)),
        ),
        compiler_params=_cparams(("parallel",)),
        name="merge_ln1_router",
    )(o_fox_r, o_fox_m, o_mla_r, o_mla_m, g, h, wfo, wmo, wout, lng, lnb, rwh, rwl, rb)


def _moe_kernel(cnt_ref, h_ref, comb_ref, gid_ref, wg_ref, wu_ref, wd_ref, lng_ref, lnb_ref, o_ref,
                p_ref, pt_ref, xs_ref, cws_ref, ys_ref, *, alpha):
    i = pl.program_id(0)
    g = pl.program_id(1)
    tm, slots = h_ref.shape[0], p_ref.shape[0]

    def chunks(gg):
        return lax.shift_right_logical(cnt_ref[i * N_GROUPS + gg] + (MOE_CHUNK - 1), MOE_CHUNK_LOG2)

    @pl.when(g == 0)
    def _():
        gid = gid_ref[...]
        sub = lax.broadcasted_iota(jnp.int32, (8, tm), 0)
        lane = lax.broadcasted_iota(jnp.int32, (8, tm), 1)
        onehot = jnp.where(gid == sub, 1, 0)
        cum = onehot
        shift = 1
        while shift < tm:
            cum = cum + jnp.where(lane >= shift, pltpu.roll(cum, shift, 1), 0)
            shift *= 2
        slot = jnp.zeros((1, tm), jnp.int32)
        off = jnp.int32(0)
        for gg in range(N_GROUPS):
            slot = slot + onehot[gg:gg + 1, :] * (cum[gg:gg + 1, :] - 1 + off)
            off = off + chunks(gg) * MOE_CHUNK
        srow = lax.broadcasted_iota(jnp.int32, (slots, tm), 0)
        p = jnp.where(srow == slot, 1.0, 0.0).astype(BF16)
        p_ref[...] = p
        stack = jnp.concatenate([comb_ref[...], slot.astype(F32),
                                 jnp.zeros((LANES - N_EXPERTS - 1, tm), F32)], axis=0)
        nat = stack.T
        lane_n = lax.broadcasted_iota(jnp.int32, (tm, LANES), 1)
        cw = jnp.where(lane_n < N_EXPERTS, nat, 0.0)
        slot_col = jnp.sum(jnp.where(lane_n == N_EXPERTS, nat, 0.0), axis=1, keepdims=True)
        scol = lax.broadcasted_iota(jnp.int32, (tm, slots), 1)
        pt_ref[...] = jnp.where(scol == slot_col.astype(jnp.int32), 1.0, 0.0).astype(BF16)
        xs_ref[...] = jnp.dot(p, h_ref[...].astype(BF16), preferred_element_type=F32).astype(BF16)
        cw_hi = cw.astype(BF16)
        cw_lo = (cw - cw_hi.astype(F32)).astype(BF16)
        both = jnp.dot(p, jnp.concatenate([cw_hi, cw_lo], axis=1), preferred_element_type=F32)
        cws_ref[...] = both[:, :LANES] + both[:, LANES:]
        ys_ref[...] = jnp.zeros_like(ys_ref)

    off_g = jnp.int32(0)
    for gg in range(N_GROUPS - 1):
        off_g = off_g + jnp.where(gg < g, chunks(gg), 0) * MOE_CHUNK
    lane_c = lax.broadcasted_iota(jnp.int32, (MOE_CHUNK, LANES), 1)

    def chunk(k, carry):
        r0 = pl.multiple_of(off_g + k * MOE_CHUNK, MOE_CHUNK)
        x = xs_ref[pl.ds(r0, MOE_CHUNK), :]
        gate = jnp.dot(x, wg_ref[...], preferred_element_type=F32)
        up = jnp.dot(x, wu_ref[...], preferred_element_type=F32)
        hid = gate * jax.nn.sigmoid(gate) * up
        cw = cws_ref[pl.ds(r0, MOE_CHUNK), :]
        pieces = []
        for e in range(EXPERTS_PER_GROUP):
            w_e = jnp.sum(jnp.where(lane_c == g * EXPERTS_PER_GROUP + e, cw, 0.0), axis=1, keepdims=True)
            pieces.append((hid[:, e * D_EXPERT:(e + 1) * D_EXPERT] * w_e).astype(BF16))
        y = jnp.dot(jnp.concatenate(pieces, axis=1), wd_ref[...], preferred_element_type=F32)
        ys_ref[pl.ds(r0, MOE_CHUNK), :] = y.astype(BF16)
        return carry

    lax.fori_loop(0, chunks(g), chunk, 0)

    @pl.when(g == N_GROUPS - 1)
    def _():
        y = jnp.dot(pt_ref[...], ys_ref[...], preferred_element_type=F32)
        o_ref[...] = _layer_norm(alpha * h_ref[...] + y, lng_ref[...], lnb_ref[...])


def _moe(h1, comb_t, gid, wg, wu, wd, lng, lnb, alpha, n_tiles):
    tm = TM_MOE
    slots = tm + N_GROUPS * MOE_CHUNK
    hidden = EXPERTS_PER_GROUP * D_EXPERT
    tile_gid = gid[0, :n_tiles * tm].reshape(n_tiles, tm)
    cnt = jnp.sum(tile_gid[:, :, None] == jnp.arange(N_GROUPS, dtype=jnp.int32), axis=1,
                  dtype=jnp.int32).reshape(-1)
    return pl.pallas_call(
        functools.partial(_moe_kernel, alpha=alpha),
        out_shape=jax.ShapeDtypeStruct((n_tiles * tm, D_MODEL), F32),
        grid_spec=pltpu.PrefetchScalarGridSpec(
            num_scalar_prefetch=1,
            grid=(n_tiles, N_GROUPS),
            in_specs=[
                pl.BlockSpec((tm, D_MODEL), lambda i, g, cnt: (i, 0)),
                pl.BlockSpec((N_EXPERTS, tm), lambda i, g, cnt: (0, i)),
                pl.BlockSpec((1, tm), lambda i, g, cnt: (0, i)),
                pl.BlockSpec((D_MODEL, hidden), lambda i, g, cnt: (0, g)),
                pl.BlockSpec((D_MODEL, hidden), lambda i, g, cnt: (0, g)),
                pl.BlockSpec((hidden, D_MODEL), lambda i, g, cnt: (g, 0)),
                pl.BlockSpec((1, D_MODEL), lambda i, g, cnt: (0, 0)),
                pl.BlockSpec((1, D_MODEL), lambda i, g, cnt: (0, 0)),
            ],
            out_specs=pl.BlockSpec((tm, D_MODEL), lambda i, g, cnt: (i, 0)),
            scratch_shapes=[
                pltpu.VMEM((slots, tm), BF16),
                pltpu.VMEM((tm, slots), BF16),
                pltpu.VMEM((slots, D_MODEL), BF16),
                pltpu.VMEM((slots, LANES), F32),
                pltpu.VMEM((slots, D_MODEL), BF16),
            ],
        ),
        compiler_params=_cparams(("parallel", "arbitrary")),
        name="moe_ln2",
    )(cnt, h1, comb_t, gid, wg, wu, wd, lng, lnb)


def _rope_tables(seq):
    half = MLA_ROPE_DIM // 2
    blk = jnp.arange(T, dtype=jnp.int32) % BLOCK
    pos = jnp.concatenate([jnp.arange(seq, dtype=jnp.int32) + N_META,
                           jnp.where(blk < N_META, blk, 0)]).astype(F32)
    rows = seq + T
    inv = ROPE_THETA ** (-jnp.arange(half, dtype=F32) / half)
    ang = pos[:, None] * inv[None, :]
    cos, sin = jnp.cos(ang), jnp.sin(ang)
    ones = jnp.ones((rows, KR_LANE), F32)
    tail = LANES - KR_LANE - MLA_ROPE_DIM
    cos_t = jnp.concatenate([ones, cos, cos, jnp.ones((rows, tail), F32)], axis=1)
    sin_t = jnp.concatenate([0 * ones, sin, sin, jnp.zeros((rows, tail), F32)], axis=1)
    return cos_t, sin_t, cos.T, sin.T


def _pad_heads(w, n_heads, per_head, keep_lo, keep_hi):
    k = w.shape[0]
    w = w.reshape(k, n_heads, per_head)[:, :, keep_lo:keep_hi]
    w = jnp.pad(w, ((0, 0), (0, 0), (0, LANES - (keep_hi - keep_lo))))
    return w.reshape(k, n_heads * LANES)


def kernel(x, meta_tokens, ln_in_g, ln_in_b, w_in, fox_f_bias, fox_w_o, mla_q_norm, mla_w_uq,
           mla_kv_norm, mla_w_ukv, mla_w_o, w_out, ln1_g, ln1_b, router_w, router_b,
           w_gate, w_up, w_down, ln2_g, ln2_b):
    batch, seq, _ = x.shape
    depth = w_in.shape[0]
    assert seq % T == 0 and batch % META_PER_TILE == 0
    tpb = seq // T
    n_real = batch * tpb
    n_tiles = n_real + batch // META_PER_TILE
    assert (n_real * T) % TM_MOE == 0 and (n_tiles * T) % TM_MOE == 0
    alpha = (2 * depth) ** 0.25
    row = lambda a: a.reshape(1, -1).astype(F32)

    h = _ln_in(x.reshape(batch * seq, D_MODEL), meta_tokens.astype(F32), row(ln_in_g), row(ln_in_b),
               n_real, n_tiles)
    tables = _rope_tables(seq)
    pqt, pk = _decay_placement()
    rw_t = router_w.T.astype(F32)
    rw_hi = rw_t.astype(BF16)
    rw_lo = (rw_t - rw_hi.astype(F32)).astype(BF16)
    rb = router_b.reshape(N_EXPERTS, 1).astype(F32)

    o_q = FOX_WIDTH
    o_k = o_q + FOX_WIDTH
    o_v = o_k + FOX_WIDTH
    o_f = o_v + FOX_HEADS
    o_cq = o_f + MLA_Q_RANK
    o_ckv = o_cq + MLA_KV_RANK
    o_kr = o_ckv + MLA_ROPE_DIM
    for i in range(depth):
        w = w_in[i]
        zeros = lambda n: jnp.zeros((D_MODEL, n), w.dtype)
        w_misc = jnp.concatenate([w[:, o_v:o_f], zeros(KR_LANE - FOX_HEADS), w[:, o_ckv:o_kr],
                                  zeros(LANES - KR_LANE - MLA_ROPE_DIM)], axis=1)
        w_k = _pad_heads(w[:, o_q:o_k], FOX_HEADS, FOX_HEAD_DIM, 0, FOX_HEAD_DIM)
        w_big = jnp.concatenate([w_k, w[:, o_f:o_ckv], w[:, o_kr:], w_misc], axis=1).astype(BF16)
        wqt = w[:, :o_q].T.astype(BF16)
        wvt = w[:, o_k:o_v].T.astype(BF16)
        bias_row = jnp.pad(fox_f_bias[i].astype(F32), (0, LANES - FOX_HEADS)).reshape(1, LANES)
        qt_f, k_f, vt_f, cq, ckv, g, misc = _proj(h, wqt, wvt, w_big, pqt, pk, bias_row, n_real, tpb)
        o_fox = _attention(qt_f, k_f, vt_f, batch, n_real, tpb, "fox_attn")

        qk_dim = MLA_NOPE_DIM + MLA_ROPE_DIM
        wuqt = _pad_heads(mla_w_uq[i], MLA_HEADS, qk_dim, 0, qk_dim).T.astype(BF16)
        kv_dim = MLA_NOPE_DIM + MLA_V_DIM
        wuk = _pad_heads(mla_w_ukv[i], MLA_HEADS, kv_dim, 0, MLA_NOPE_DIM).astype(BF16)
        wuv = mla_w_ukv[i].reshape(MLA_KV_RANK, MLA_HEADS, kv_dim)[:, :, MLA_NOPE_DIM:]
        wuvt = wuv.reshape(MLA_KV_RANK, MLA_WIDTH).T.astype(BF16)
        qt_m, k_m, vt_m = _mla_prep(cq, ckv, misc, tables, row(mla_q_norm[i]), row(mla_kv_norm[i]),
                                    wuqt, wuk, wuvt, n_real, tpb)
        o_mla = _attention(qt_m, k_m, vt_m, batch, n_real, tpb, "mla_attn")

        h1, comb_t, gid = _post(o_fox, o_mla, g, h, fox_w_o[i].astype(BF16), mla_w_o[i].astype(BF16),
                                w_out[i].astype(BF16), row(ln1_g[i]), row(ln1_b[i]), rw_hi, rw_lo, rb,
                                alpha)

        wg = jnp.transpose(w_gate[i], (1, 0, 2)).reshape(D_MODEL, N_EXPERTS * D_EXPERT).astype(BF16)
        wu = jnp.transpose(w_up[i], (1, 0, 2)).reshape(D_MODEL, N_EXPERTS * D_EXPERT).astype(BF16)
        wd = w_down[i].reshape(N_EXPERTS * D_EXPERT, D_MODEL).astype(BF16)
        moe_rows = (n_real if i == depth - 1 else n_tiles) * T
        h = _moe(h1, comb_t, gid, wg, wu, wd, row(ln2_g[i]), row(ln2_b[i]), alpha, moe_rows // TM_MOE)

    return h.reshape(batch, seq, D_MODEL)
```

```python
import functools
import math

import jax
import jax.numpy as jnp
from jax import lax
from jax.experimental import pallas as pl
from jax.experimental.pallas import tpu as pltpu

F32 = jnp.float32
BF16 = jnp.bfloat16

D_MODEL = 1024
N_META = 16
BLOCK = 128
NEG_INF = -1e30
LOG2E = math.log2(math.e)

FOX_HEADS = 8
FOX_HEAD_DIM = 64
FOX_WIDTH = FOX_HEADS * FOX_HEAD_DIM

MLA_HEADS = 8
MLA_NOPE_DIM = 64
MLA_ROPE_DIM = 32
MLA_V_DIM = 64
MLA_Q_RANK = 384
MLA_KV_RANK = 256
MLA_WIDTH = MLA_HEADS * MLA_V_DIM
ROPE_THETA = 10000.0

N_EXPERTS = 16
N_GROUPS = 4
EXPERTS_PER_GROUP = N_EXPERTS // N_GROUPS
D_EXPERT = 256

LN_EPS = 1e-5
RMS_EPS = 1e-6

LANES = 128
N_HEADS = FOX_HEADS
ATTN_HEADS = 4
WIDE = N_HEADS * LANES
KR_LANE = 64

E_HI, E_MID, E_LO, E_ONE, E_PAD = 0, N_HEADS, 2 * N_HEADS, 3 * N_HEADS, 3 * N_HEADS + 1
X_CQ, X_ONE_K, X_BIAS = FOX_HEAD_DIM, FOX_HEAD_DIM + 3, FOX_HEAD_DIM + 6
M_BIAS = MLA_NOPE_DIM + MLA_ROPE_DIM

C_K = 0
C_CQ = WIDE
C_CKV = C_CQ + MLA_Q_RANK
C_G = C_CKV + MLA_KV_RANK
C_MISC = C_G + 2 * D_MODEL
PROJ_COLS = C_MISC + LANES

VMEM_LIMIT = 56 * 1024 * 1024

T = 512
META_PER_TILE = T // BLOCK
TM_POST = 512
TM_MOE = 1024
MOE_CHUNK_LOG2 = 7
MOE_CHUNK = 1 << MOE_CHUNK_LOG2

NT_DIMS = (((1,), (1,)), ((), ()))


def _cparams(sem):
    return pltpu.CompilerParams(dimension_semantics=sem, vmem_limit_bytes=VMEM_LIMIT)


def _layer_norm(x, g, b):
    mu = jnp.mean(x, axis=-1, keepdims=True)
    xc = x - mu
    var = jnp.mean(xc * xc, axis=-1, keepdims=True)
    return xc * lax.rsqrt(var + LN_EPS) * g + b


def _rms_norm(x, g):
    ms = jnp.mean(x * x, axis=-1, keepdims=True)
    return x * lax.rsqrt(ms + RMS_EPS) * g


def _ln_in_kernel(x_ref, meta_ref, g_ref, b_ref, o_ref, *, n_real):
    i = pl.program_id(0)

    @pl.when(i < n_real)
    def _():
        o_ref[...] = _layer_norm(x_ref[...], g_ref[...], b_ref[...])

    @pl.when(i >= n_real)
    def _():
        o_ref[...] = jnp.zeros_like(o_ref)
        m = _layer_norm(meta_ref[...], g_ref[...], b_ref[...])
        for jb in range(META_PER_TILE):
            o_ref[jb * BLOCK:jb * BLOCK + N_META, :] = m


def _ln_in(x2d, meta, g, b, n_real, n_tiles):
    full = lambda i: (0, 0)
    return pl.pallas_call(
        functools.partial(_ln_in_kernel, n_real=n_real),
        out_shape=jax.ShapeDtypeStruct((n_tiles * T, D_MODEL), F32),
        grid=(n_tiles,),
        in_specs=[
            pl.BlockSpec((T, D_MODEL), lambda i: (jnp.minimum(i, n_real - 1), 0)),
            pl.BlockSpec((N_META, D_MODEL), full),
            pl.BlockSpec((1, D_MODEL), full),
            pl.BlockSpec((1, D_MODEL), full),
        ],
        out_specs=pl.BlockSpec((T, D_MODEL), lambda i: (i, 0)),
        compiler_params=_cparams(("parallel",)),
        name="ln_in",
    )(x2d, meta, g, b)


def _proj_kernel(x_ref, wqt_ref, wvt_ref, w_ref, pqt_ref, pk_ref, bias_ref,
                 cos_ref, sin_ref, cost_ref, sint_ref, qn_ref, kvn_ref, wuqt_ref, wuk_ref, wuvt_ref,
                 qt_ref, k_ref, vt_ref, g_ref, qtm_ref, km_ref, vtm_ref, carry_ref, c_ref,
                 *, n_real, tpb):
    i = pl.program_id(0)
    x = x_ref[...].astype(BF16)

    def mm(lo, hi):
        return jnp.dot(x, w_ref[:, lo:hi], preferred_element_type=F32)

    misc = mm(C_MISC, PROJ_COLS)
    g_ref[:, :D_MODEL] = mm(C_G, C_G + D_MODEL).astype(BF16)
    g_ref[:, D_MODEL:] = mm(C_G + D_MODEL, C_MISC).astype(BF16)

    z = misc + bias_ref[...]
    logf = jnp.minimum(z, 0.0) - jnp.log1p(jnp.exp(-jnp.abs(z)))
    row = lax.broadcasted_iota(jnp.int32, (T, LANES), 0)
    lane = lax.broadcasted_iota(jnp.int32, (T, LANES), 1)
    blk_row = row % BLOCK
    is_meta = i >= n_real

    @pl.when(jnp.logical_not(is_meta))
    def _():
        @pl.when(i % tpb == 0)
        def _():
            carry_ref[...] = jnp.zeros_like(carry_ref)

        c = logf
        shift = 1
        while shift < T:
            c = c + jnp.where(row >= shift, pltpu.roll(c, shift, 0), 0.0)
            shift *= 2
        c = c + carry_ref[...]
        carry_ref[...] = c[T - 1:T, :]
        c_ref[...] = c

    @pl.when(is_meta)
    def _():
        own = jnp.where(blk_row < N_META, logf, 0.0)
        s = own
        shift = 1
        while shift < BLOCK:
            s = s + jnp.where(blk_row < BLOCK - shift, pltpu.roll(s, T - shift, 0), 0.0)
            shift *= 2
        c_ref[...] = own - s

    is_pad = is_meta & (blk_row >= N_META)
    _mla_operands(mm(C_CQ, C_CKV), mm(C_CKV, C_G), misc, is_pad, cos_ref[...], sin_ref[...],
                  cost_ref[...], sint_ref[...], qn_ref[...], kvn_ref[...], wuqt_ref[...],
                  wuk_ref[...], wuvt_ref[...], qtm_ref, km_ref, vtm_ref)
    c2 = c_ref[...] * LOG2E
    hi = c2.astype(BF16).astype(F32)
    r1 = c2 - hi
    mid = r1.astype(BF16).astype(F32)
    lo = (r1 - mid).astype(BF16).astype(F32)
    feat = jnp.where(lane < E_MID, hi,
           jnp.where(lane < E_LO, pltpu.roll(mid, E_MID, 1),
           jnp.where(lane < E_ONE, pltpu.roll(lo, E_LO, 1),
           jnp.where(lane == E_ONE, 1.0,
           jnp.where((lane == E_PAD) & is_pad, 1.0, 0.0)))))
    feat = feat.astype(BF16)
    extra_k = jnp.dot(feat, pk_ref[...], preferred_element_type=F32)
    extra_qt = lax.dot_general(pqt_ref[...], feat, NT_DIMS, preferred_element_type=F32)

    k_ref[0] = (mm(C_K, C_CQ) + extra_k).astype(BF16)
    qt = lax.dot_general(wqt_ref[...], x, NT_DIMS, preferred_element_type=F32)
    qt = qt * (FOX_HEAD_DIM ** -0.5 * LOG2E)
    for h in range(N_HEADS):
        qt_ref[0, h * LANES:h * LANES + FOX_HEAD_DIM, :] = (
            qt[h * FOX_HEAD_DIM:(h + 1) * FOX_HEAD_DIM, :].astype(BF16))
        qt_ref[0, h * LANES + FOX_HEAD_DIM:(h + 1) * LANES, :] = (
            extra_qt[h * LANES + FOX_HEAD_DIM:(h + 1) * LANES, :].astype(BF16))
    vt_ref[0] = lax.dot_general(wvt_ref[...], x, NT_DIMS, preferred_element_type=F32).astype(BF16)


def _proj(h, wqt, wvt, w_big, pqt, pk, bias_row, tables, qn, kvn, wuqt, wuk, wuvt, n_real, tpb):
    rows = h.shape[0]
    nt = rows // T
    cos_t, sin_t, cos_tt, sin_tt = tables
    half = MLA_ROPE_DIM // 2
    row = lambda i: (i, 0)
    full = lambda i: (0, 0)
    blk = lambda i: (i, 0, 0)
    tab = lambda i: jnp.where(i < n_real, i % tpb, tpb)
    operands = (
        jax.ShapeDtypeStruct((nt, WIDE, T), BF16),
        jax.ShapeDtypeStruct((nt, T, WIDE), BF16),
        jax.ShapeDtypeStruct((nt, N_HEADS * MLA_V_DIM, T), BF16),
    )
    operand_specs = (
        pl.BlockSpec((1, WIDE, T), blk),
        pl.BlockSpec((1, T, WIDE), blk),
        pl.BlockSpec((1, N_HEADS * MLA_V_DIM, T), blk),
    )
    return pl.pallas_call(
        functools.partial(_proj_kernel, n_real=n_real, tpb=tpb),
        out_shape=operands + (jax.ShapeDtypeStruct((rows, 2 * D_MODEL), BF16),) + operands,
        grid=(nt,),
        in_specs=[
            pl.BlockSpec((T, D_MODEL), row),
            pl.BlockSpec((FOX_WIDTH, D_MODEL), full),
            pl.BlockSpec((FOX_WIDTH, D_MODEL), full),
            pl.BlockSpec((D_MODEL, PROJ_COLS), full),
            pl.BlockSpec((WIDE, LANES), full),
            pl.BlockSpec((LANES, WIDE), full),
            pl.BlockSpec((1, LANES), full),
            pl.BlockSpec((T, LANES), lambda i: (tab(i), 0)),
            pl.BlockSpec((T, LANES), lambda i: (tab(i), 0)),
            pl.BlockSpec((half, T), lambda i: (0, tab(i))),
            pl.BlockSpec((half, T), lambda i: (0, tab(i))),
            pl.BlockSpec((1, MLA_Q_RANK), full),
            pl.BlockSpec((1, MLA_KV_RANK), full),
            pl.BlockSpec((WIDE, MLA_Q_RANK), full),
            pl.BlockSpec((MLA_KV_RANK, WIDE), full),
            pl.BlockSpec((MLA_WIDTH, MLA_KV_RANK), full),
        ],
        out_specs=operand_specs + (pl.BlockSpec((T, 2 * D_MODEL), row),) + operand_specs,
        scratch_shapes=[pltpu.VMEM((1, LANES), F32), pltpu.VMEM((T, LANES), F32)],
        compiler_params=_cparams(("arbitrary",)),
        name="in_proj",
    )(h, wqt, wvt, w_big, pqt, pk, bias_row, cos_t, sin_t, cos_tt, sin_tt, qn, kvn, wuqt, wuk, wuvt)


def _decay_placement():
    pk = [[0.0] * WIDE for _ in range(LANES)]
    pqt = [[0.0] * LANES for _ in range(WIDE)]
    for h in range(N_HEADS):
        base = h * LANES
        for s, e in enumerate((E_HI, E_MID, E_LO)):
            pqt[base + X_CQ + s][e + h] = 1.0
            pk[E_ONE][base + X_CQ + s] = 1.0
            pqt[base + X_ONE_K + s][E_ONE] = 1.0
            pk[e + h][base + X_ONE_K + s] = -1.0
        pqt[base + X_BIAS][E_ONE] = 1.0
        pk[E_PAD][base + X_BIAS] = NEG_INF
    return jnp.array(pqt, F32).astype(BF16), jnp.array(pk, F32).astype(BF16)


def _attn_kernel(qt_ref, k_ref, vt_ref, qtm_ref, km_ref, vtm_ref, o_ref, om_ref, st_ref, stm_ref):
    qi = pl.program_id(2)
    nh = ATTN_HEADS
    qt = qt_ref[0]
    km = km_ref[0]
    vtm = vtm_ref[0]

    def causal(n):
        return (lax.broadcasted_iota(jnp.int32, (n, n), 0)
                <= lax.broadcasted_iota(jnp.int32, (n, n), 1))

    def head(a, jj, width):
        return a[jj * width:(jj + 1) * width]

    @pl.when(qi == 0)
    def _():
        qtm = qtm_ref[0]
        outs = []
        for jj in range(nh):
            st = jnp.dot(km[:, jj * LANES:(jj + 1) * LANES], head(qtm, jj, LANES),
                         preferred_element_type=F32)
            st = jnp.where(causal(BLOCK), st, NEG_INF)
            p = jnp.exp2(st - jnp.max(st, axis=0, keepdims=True))
            pv = jnp.dot(head(vtm, jj, MLA_V_DIM), p.astype(BF16), preferred_element_type=F32)
            outs.append(pv / jnp.sum(p, axis=0, keepdims=True))
        om_ref[...] = jnp.concatenate(outs, axis=0).T.astype(om_ref.dtype)

    def scores(kj, slot):
        k = k_ref[kj]
        for jj in range(nh):
            st_ref[slot, jj] = jnp.dot(k[:, jj * LANES:(jj + 1) * LANES], head(qt, jj, LANES),
                                       preferred_element_type=F32)

    def consume(kj, slot, state, diagonal):
        vt = vt_ref[kj]
        out = []
        for jj in range(nh):
            m, l, acc = state[jj]
            st = st_ref[slot, jj]
            m_new = m
            if diagonal:
                st = jnp.where(causal(T), st, NEG_INF)
                stm = stm_ref[jj]
                m_new = jnp.maximum(m_new, jnp.max(stm, axis=0, keepdims=True))
            m_new = jnp.maximum(m_new, jnp.max(st, axis=0, keepdims=True))
            alpha = jnp.exp2(m - m_new)
            p = jnp.exp2(st - m_new)
            l_new = alpha * l + jnp.sum(p, axis=0, keepdims=True)
            pv = jnp.dot(head(vt, jj, MLA_V_DIM), p.astype(BF16), preferred_element_type=F32)
            if diagonal:
                pm = jnp.exp2(stm - m_new)
                l_new = l_new + jnp.sum(pm, axis=0, keepdims=True)
                pv = pv + jnp.dot(head(vtm, jj, MLA_V_DIM), pm.astype(BF16),
                                  preferred_element_type=F32)
            out.append((m_new, l_new, alpha * acc + pv))
        return tuple(out)

    init_one = (jnp.full((1, T), NEG_INF, F32), jnp.zeros((1, T), F32),
                jnp.zeros((MLA_V_DIM, T), F32))
    state = (init_one,) * nh

    def pair(i, state):
        c0 = 2 * i
        scores(c0 + 1, 1)
        state = consume(c0, 0, state, False)
        scores(c0 + 2, 0)
        return consume(c0 + 1, 1, state, False)

    def odd_tail(state):
        scores(qi, 1)
        state = consume(qi - 1, 0, state, False)
        return consume(qi, 1, state, True)

    def even_tail(state):
        return consume(qi, 0, state, True)

    scores(0, 0)
    for jj in range(nh):
        stm_ref[jj] = jnp.dot(km[:, jj * LANES:(jj + 1) * LANES], head(qt, jj, LANES),
                              preferred_element_type=F32)
    state = lax.fori_loop(0, qi // 2, pair, state)
    state = lax.cond(qi % 2 == 1, odd_tail, even_tail, state)
    ot = jnp.concatenate([acc / l for (_, l, acc) in state], axis=0)
    o_ref[...] = ot.T.astype(o_ref.dtype)


def _attention(qt, k, vt, batch, n_real, tpb, name):
    nh = ATTN_HEADS
    qk_w, v_w = nh * LANES, nh * MLA_V_DIM
    meta_tile = lambda b: n_real + b // META_PER_TILE
    meta_blk = lambda b: b % META_PER_TILE
    return pl.pallas_call(
        _attn_kernel,
        out_shape=(
            jax.ShapeDtypeStruct((n_real * T, N_HEADS * MLA_V_DIM), BF16),
            jax.ShapeDtypeStruct((batch * BLOCK, N_HEADS * MLA_V_DIM), BF16),
        ),
        grid=(batch, N_HEADS // nh, tpb),
        in_specs=[
            pl.BlockSpec((1, qk_w, T), lambda b, hg, qi: (b * tpb + qi, hg, 0)),
            pl.BlockSpec((tpb, T, qk_w), lambda b, hg, qi: (b, 0, hg)),
            pl.BlockSpec((tpb, v_w, T), lambda b, hg, qi: (b, hg, 0)),
            pl.BlockSpec((1, qk_w, BLOCK), lambda b, hg, qi: (meta_tile(b), hg, meta_blk(b))),
            pl.BlockSpec((1, BLOCK, qk_w), lambda b, hg, qi: (meta_tile(b), meta_blk(b), hg)),
            pl.BlockSpec((1, v_w, BLOCK), lambda b, hg, qi: (meta_tile(b), hg, meta_blk(b))),
        ],
        out_specs=(
            pl.BlockSpec((T, v_w), lambda b, hg, qi: (b * tpb + qi, hg)),
            pl.BlockSpec((BLOCK, v_w), lambda b, hg, qi: (b, hg)),
        ),
        scratch_shapes=[pltpu.VMEM((2, nh, T, T), F32),
                        pltpu.VMEM((nh, BLOCK, T), F32)],
        compiler_params=_cparams(("parallel", "parallel", "arbitrary")),
        name=name,
    )(qt, k, vt, qt, k, vt)


def _mla_operands(cq, ckv, misc, is_pad, cos, sin, cost, sint, qn_g, kvn_g, wuqt, wuk, wuvt,
                  qt_ref, k_ref, vt_ref):
    half = MLA_ROPE_DIM // 2

    qn = _rms_norm(cq, qn_g).astype(BF16)
    qt = lax.dot_general(wuqt, qn, NT_DIMS, preferred_element_type=F32)
    qt = qt * ((MLA_NOPE_DIM + MLA_ROPE_DIM) ** -0.5 * LOG2E)
    tail =jnp.where(lax.broadcasted_iota(jnp.int32, (LANES - M_BIAS, T), 0) == 0, 1.0, 0.0)
    for h in range(N_HEADS):
        base = h * LANES
        x1 = qt[base + MLA_NOPE_DIM:base + MLA_NOPE_DIM + half, :]
        x2 = qt[base + MLA_NOPE_DIM + half:base + M_BIAS, :]
        qt_ref[0, base:base + MLA_NOPE_DIM, :] = qt[base:base + MLA_NOPE_DIM, :].astype(BF16)
        qt_ref[0, base + MLA_NOPE_DIM:base + MLA_NOPE_DIM + half, :] = (x1 * cost - x2 * sint).astype(BF16)
        qt_ref[0, base + MLA_NOPE_DIM + half:base + M_BIAS, :] = (x2 * cost + x1 * sint).astype(BF16)
        qt_ref[0, base + M_BIAS:base + LANES, :] = tail.astype(BF16)

    kvn = _rms_norm(ckv, kvn_g).astype(BF16)
    k_nope = jnp.dot(kvn, wuk, preferred_element_type=F32)
    lane = lax.broadcasted_iota(jnp.int32, (T, LANES), 1)
    kr = jnp.where((lane >= KR_LANE) & (lane < KR_LANE + MLA_ROPE_DIM), misc, 0.0)
    from_hi = pltpu.roll(kr, LANES - half, 1)
    from_lo = pltpu.roll(kr, half, 1)
    swapped = jnp.where(lane < KR_LANE + half, -from_hi, from_lo)
    k_rot = kr * cos + swapped * sin
    k_rot = jnp.where((lane == M_BIAS) & is_pad, NEG_INF, k_rot)
    k_ref[0] = (k_nope + jnp.concatenate([k_rot] * N_HEADS, axis=1)).astype(BF16)
    vt_ref[0] = lax.dot_general(wuvt, kvn, NT_DIMS, preferred_element_type=F32).astype(BF16)


def _top2_sum(a, b, c, d):
    hi1, lo1 = jnp.maximum(a, b), jnp.minimum(a, b)
    hi2, lo2 = jnp.maximum(c, d), jnp.minimum(c, d)
    return jnp.maximum(hi1, hi2) + jnp.maximum(jnp.minimum(hi1, hi2), jnp.maximum(lo1, lo2))


def _route(logits_t, bias_col):
    scores = jax.nn.sigmoid(logits_t)
    biased = scores + bias_col
    b = [biased[e:e + 1, :] for e in range(N_EXPERTS)]
    s = [scores[e:e + 1, :] for e in range(N_EXPERTS)]
    gscore = [_top2_sum(*b[EXPERTS_PER_GROUP * g:EXPERTS_PER_GROUP * (g + 1)]) for g in range(N_GROUPS)]
    best = gscore[0]
    gidx = jnp.zeros_like(best, dtype=jnp.int32)
    for g in range(1, N_GROUPS):
        better = gscore[g] > best
        gidx = jnp.where(better, g, gidx)
        best = jnp.where(better, gscore[g], best)
    in_g = [gidx == g for g in range(N_GROUPS)]

    def pick(vals, j):
        out = vals[j]
        for g in range(1, N_GROUPS):
            out = jnp.where(in_g[g], vals[EXPERTS_PER_GROUP * g + j], out)
        return out

    vb = [pick(b, j) for j in range(EXPERTS_PER_GROUP)]
    vs = [pick(s, j) for j in range(EXPERTS_PER_GROUP)]
    chosen = []
    for j in range(EXPERTS_PER_GROUP):
        rank = jnp.zeros_like(gidx)
        for i in range(EXPERTS_PER_GROUP):
            if i == j:
                continue
            ahead = (vb[i] >= vb[j]) if i < j else (vb[i] > vb[j])
            rank = rank + jnp.where(ahead, 1, 0)
        chosen.append(rank < 2)
    total = sum(jnp.where(chosen[j], vs[j], 0.0) for j in range(EXPERTS_PER_GROUP))
    gates = [jnp.where(chosen[j], vs[j] / total, 0.0) for j in range(EXPERTS_PER_GROUP)]
    rows = []
    for g in range(N_GROUPS):
        for j in range(EXPERTS_PER_GROUP):
            rows.append(jnp.where(in_g[g], gates[j], 0.0))
    return jnp.concatenate(rows, axis=0), gidx


def _post_kernel(of_ref, ofm_ref, om_ref, omm_ref, g_ref, h_ref, wfo_ref, wmo_ref, wout_ref,
                 lng_ref, lnb_ref, rwh_ref, rwl_ref, rb_ref, h1_ref, comb_ref, gid_ref,
                 *, alpha, n_real_tiles):
    is_meta = pl.program_id(0) >= n_real_tiles
    o_fox = jnp.where(is_meta, ofm_ref[...], of_ref[...])
    o_mla = jnp.where(is_meta, omm_ref[...], om_ref[...])
    y_fox = jnp.dot(o_fox, wfo_ref[...], preferred_element_type=F32)
    y_mla = jnp.dot(o_mla, wmo_ref[...], preferred_element_type=F32)
    merged = (jax.nn.sigmoid(g_ref[:, :D_MODEL].astype(F32)) * y_fox
              + jax.nn.sigmoid(g_ref[:, D_MODEL:].astype(F32)) * y_mla)
    mix = jnp.dot(merged.astype(BF16), wout_ref[...], preferred_element_type=F32)
    h1 = _layer_norm(alpha * h_ref[...] + mix, lng_ref[...], lnb_ref[...])
    h1_ref[...] = h1
    h_hi = h1.astype(BF16)
    h_lo = (h1 - h_hi.astype(F32)).astype(BF16)
    rwh, rwl = rwh_ref[...], rwl_ref[...]
    logits_t = (lax.dot_general(rwh, h_hi, NT_DIMS, preferred_element_type=F32)
                + lax.dot_general(rwl, h_hi, NT_DIMS, preferred_element_type=F32)
                + lax.dot_general(rwh, h_lo, NT_DIMS, preferred_element_type=F32))
    comb_ref[...], gid_ref[...] = _route(logits_t, rb_ref[...])


def _post(o_fox, o_mla, g, h, wfo, wmo, wout, lng, lnb, rwh, rwl, rb, alpha):
    rows = h.shape[0]
    tm = TM_POST
    o_fox_r, o_fox_m = o_fox
    o_mla_r, o_mla_m = o_mla
    nrt = o_fox_r.shape[0] // tm
    row = lambda i: (i, 0)
    real = lambda i: (jnp.minimum(i, nrt - 1), 0)
    meta = lambda i: (jnp.maximum(i - nrt, 0), 0)
    full = lambda i: (0, 0)
    return pl.pallas_call(
        functools.partial(_post_kernel, alpha=alpha, n_real_tiles=nrt),
        out_shape=(
            jax.ShapeDtypeStruct((rows, D_MODEL), F32),
            jax.ShapeDtypeStruct((N_EXPERTS, rows), F32),
            jax.ShapeDtypeStruct((1, rows), jnp.int32),
        ),
        grid=(rows // tm,),
        in_specs=[
            pl.BlockSpec((tm, FOX_WIDTH), real),
            pl.BlockSpec((tm, FOX_WIDTH), meta),
            pl.BlockSpec((tm, MLA_WIDTH), real),
            pl.BlockSpec((tm, MLA_WIDTH), meta),
            pl.BlockSpec((tm, 2 * D_MODEL), row),
            pl.BlockSpec((tm, D_MODEL), row),
            pl.BlockSpec((FOX_WIDTH, D_MODEL), full),
            pl.BlockSpec((MLA_WIDTH, D_MODEL), full),
            pl.BlockSpec((D_MODEL, D_MODEL), full),
            pl.BlockSpec((1, D_MODEL), full),
            pl.BlockSpec((1, D_MODEL), full),
            pl.BlockSpec((N_EXPERTS, D_MODEL), full),
            pl.BlockSpec((N_EXPERTS, D_MODEL), full),
            pl.BlockSpec((N_EXPERTS, 1), full),
        ],
        out_specs=(
            pl.BlockSpec((tm, D_MODEL), row),
            pl.BlockSpec((N_EXPERTS, tm), lambda i: (0, i)),
            pl.BlockSpec((1, tm), lambda i: (0, i)),
        ),
        compiler_params=_cparams(("parallel",)),
        name="merge_ln1_router",
    )(o_fox_r, o_fox_m, o_mla_r, o_mla_m, g, h, wfo, wmo, wout, lng, lnb, rwh, rwl, rb)


def _moe_kernel(cnt_ref, h_ref, comb_ref, gid_ref, wg_ref, wu_ref, wd_ref, lng_ref, lnb_ref, o_ref,
                p_ref, pt_ref, xs_ref, cws_ref, ys_ref, *, alpha):
    i = pl.program_id(0)
    g = pl.program_id(1)
    tm, slots = h_ref.shape[0], p_ref.shape[0]

    def chunks(gg):
        return lax.shift_right_logical(cnt_ref[i * N_GROUPS + gg] + (MOE_CHUNK - 1), MOE_CHUNK_LOG2)

    @pl.when(g == 0)
    def _():
        gid = gid_ref[...]
        sub = lax.broadcasted_iota(jnp.int32, (8, tm), 0)
        lane = lax.broadcasted_iota(jnp.int32, (8, tm), 1)
        onehot = jnp.where(gid == sub, 1, 0)
        cum = onehot
        shift = 1
        while shift < tm:
            cum = cum + jnp.where(lane >= shift, pltpu.roll(cum, shift, 1), 0)
            shift *= 2
        slot = jnp.zeros((1, tm), jnp.int32)
        off = jnp.int32(0)
        for gg in range(N_GROUPS):
            slot = slot + onehot[gg:gg + 1, :] * (cum[gg:gg + 1, :] - 1 + off)
            off = off + chunks(gg) * MOE_CHUNK
        srow = lax.broadcasted_iota(jnp.int32, (slots, tm), 0)
        p = jnp.where(srow == slot, 1.0, 0.0).astype(BF16)
        p_ref[...] = p
        stack = jnp.concatenate([comb_ref[...], slot.astype(F32),
                                 jnp.zeros((LANES - N_EXPERTS - 1, tm), F32)], axis=0)
        nat = stack.T
        lane_n = lax.broadcasted_iota(jnp.int32, (tm, LANES), 1)
        cw = jnp.where(lane_n < N_EXPERTS, nat, 0.0)
        slot_col = jnp.sum(jnp.where(lane_n == N_EXPERTS, nat, 0.0), axis=1, keepdims=True)
        scol = lax.broadcasted_iota(jnp.int32, (tm, slots), 1)
        pt_ref[...] = jnp.where(scol == slot_col.astype(jnp.int32), 1.0, 0.0).astype(BF16)
        xs_ref[...] = jnp.dot(p, h_ref[...].astype(BF16), preferred_element_type=F32).astype(BF16)
        cw_hi = cw.astype(BF16)
        cw_lo = (cw - cw_hi.astype(F32)).astype(BF16)
        both = jnp.dot(p, jnp.concatenate([cw_hi, cw_lo], axis=1), preferred_element_type=F32)
        cws_ref[...] = both[:, :LANES] + both[:, LANES:]
        ys_ref[...] = jnp.zeros_like(ys_ref)

    off_g = jnp.int32(0)
    for gg in range(N_GROUPS - 1):
        off_g = off_g + jnp.where(gg < g, chunks(gg), 0) * MOE_CHUNK
    lane_c = lax.broadcasted_iota(jnp.int32, (MOE_CHUNK, LANES), 1)

    def chunk(k, carry):
        r0 = pl.multiple_of(off_g + k * MOE_CHUNK, MOE_CHUNK)
        x = xs_ref[pl.ds(r0, MOE_CHUNK), :]
        gate = jnp.dot(x, wg_ref[...], preferred_element_type=F32)
        up = jnp.dot(x, wu_ref[...], preferred_element_type=F32)
        hid = gate * jax.nn.sigmoid(gate) * up
        cw = cws_ref[pl.ds(r0, MOE_CHUNK), :]
        pieces = []
        for e in range(EXPERTS_PER_GROUP):
            w_e = jnp.sum(jnp.where(lane_c == g * EXPERTS_PER_GROUP + e, cw, 0.0), axis=1, keepdims=True)
            pieces.append((hid[:, e * D_EXPERT:(e + 1) * D_EXPERT] * w_e).astype(BF16))
        y = jnp.dot(jnp.concatenate(pieces, axis=1), wd_ref[...], preferred_element_type=F32)
        ys_ref[pl.ds(r0, MOE_CHUNK), :] = y.astype(BF16)
        return carry

    lax.fori_loop(0, chunks(g), chunk, 0)

    @pl.when(g == N_GROUPS - 1)
    def _():
        y = jnp.dot(pt_ref[...], ys_ref[...], preferred_element_type=F32)
        o_ref[...] = _layer_norm(alpha * h_ref[...] + y, lng_ref[...], lnb_ref[...])


def _moe(h1, comb_t, gid, wg, wu, wd, lng, lnb, alpha, n_tiles):
    tm = TM_MOE
    slots = tm + N_GROUPS * MOE_CHUNK
    hidden = EXPERTS_PER_GROUP * D_EXPERT
    tile_gid = gid[0, :n_tiles * tm].reshape(n_tiles, tm)
    cnt = jnp.sum(tile_gid[:, :, None] == jnp.arange(N_GROUPS, dtype=jnp.int32), axis=1,
                  dtype=jnp.int32).reshape(-1)
    return pl.pallas_call(
        functools.partial(_moe_kernel, alpha=alpha),
        out_shape=jax.ShapeDtypeStruct((n_tiles * tm, D_MODEL), F32),
        grid_spec=pltpu.PrefetchScalarGridSpec(
            num_scalar_prefetch=1,
            grid=(n_tiles, N_GROUPS),
            in_specs=[
                pl.BlockSpec((tm, D_MODEL), lambda i, g, cnt: (i, 0)),
                pl.BlockSpec((N_EXPERTS, tm), lambda i, g, cnt: (0, i)),
                pl.BlockSpec((1, tm), lambda i, g, cnt: (0, i)),
                pl.BlockSpec((D_MODEL, hidden), lambda i, g, cnt: (0, g)),
                pl.BlockSpec((D_MODEL, hidden), lambda i, g, cnt: (0, g)),
                pl.BlockSpec((hidden, D_MODEL), lambda i, g, cnt: (g, 0)),
                pl.BlockSpec((1, D_MODEL), lambda i, g, cnt: (0, 0)),
                pl.BlockSpec((1, D_MODEL), lambda i, g, cnt: (0, 0)),
            ],
            out_specs=pl.BlockSpec((tm, D_MODEL), lambda i, g, cnt: (i, 0)),
            scratch_shapes=[
                pltpu.VMEM((slots, tm), BF16),
                pltpu.VMEM((tm, slots), BF16),
                pltpu.VMEM((slots, D_MODEL), BF16),
                pltpu.VMEM((slots, LANES), F32),
                pltpu.VMEM((slots, D_MODEL), BF16),
            ],
        ),
        compiler_params=_cparams(("parallel", "arbitrary")),
        name="moe_ln2",
    )(cnt, h1, comb_t, gid, wg, wu, wd, lng, lnb)


def _rope_tables(seq):
    half = MLA_ROPE_DIM // 2
    blk = jnp.arange(T, dtype=jnp.int32) % BLOCK
    pos = jnp.concatenate([jnp.arange(seq, dtype=jnp.int32) + N_META,
                           jnp.where(blk < N_META, blk, 0)]).astype(F32)
    rows = seq + T
    inv = ROPE_THETA ** (-jnp.arange(half, dtype=F32) / half)
    ang = pos[:, None] * inv[None, :]
    cos, sin = jnp.cos(ang), jnp.sin(ang)
    ones = jnp.ones((rows, KR_LANE), F32)
    tail = LANES - KR_LANE - MLA_ROPE_DIM
    cos_t = jnp.concatenate([ones, cos, cos, jnp.ones((rows, tail), F32)], axis=1)
    sin_t = jnp.concatenate([0 * ones, sin, sin, jnp.zeros((rows, tail), F32)], axis=1)
    return cos_t, sin_t, cos.T, sin.T


def _pad_heads(w, n_heads, per_head, keep_lo, keep_hi):
    k = w.shape[0]
    w = w.reshape(k, n_heads, per_head)[:, :, keep_lo:keep_hi]
    w = jnp.pad(w, ((0, 0), (0, 0), (0, LANES - (keep_hi - keep_lo))))
    return w.reshape(k, n_heads * LANES)


def kernel(x, meta_tokens, ln_in_g, ln_in_b, w_in, fox_f_bias, fox_w_o, mla_q_norm, mla_w_uq,
           mla_kv_norm, mla_w_ukv, mla_w_o, w_out, ln1_g, ln1_b, router_w, router_b,
           w_gate, w_up, w_down, ln2_g, ln2_b):
    batch, seq, _ = x.shape
    depth = w_in.shape[0]
    assert seq % T == 0 and batch % META_PER_TILE == 0
    tpb = seq // T
    n_real = batch * tpb
    n_tiles = n_real + batch // META_PER_TILE
    assert (n_real * T) % TM_MOE == 0 and (n_tiles * T) % TM_MOE == 0
    alpha = (2 * depth) ** 0.25
    row = lambda a: a.reshape(1, -1).astype(F32)

    h = _ln_in(x.reshape(batch * seq, D_MODEL), meta_tokens.astype(F32), row(ln_in_g), row(ln_in_b),
               n_real, n_tiles)
    tables = _rope_tables(seq)
    pqt, pk = _decay_placement()
    rw_t = router_w.T.astype(F32)
    rw_hi = rw_t.astype(BF16)
    rw_lo = (rw_t - rw_hi.astype(F32)).astype(BF16)
    rb = router_b.reshape(N_EXPERTS, 1).astype(F32)

    o_q = FOX_WIDTH
    o_k = o_q + FOX_WIDTH
    o_v = o_k + FOX_WIDTH
    o_f = o_v + FOX_HEADS
    o_cq = o_f + MLA_Q_RANK
    o_ckv = o_cq + MLA_KV_RANK
    o_kr = o_ckv + MLA_ROPE_DIM
    for i in range(depth):
        w = w_in[i]
        zeros = lambda n: jnp.zeros((D_MODEL, n), w.dtype)
        w_misc = jnp.concatenate([w[:, o_v:o_f], zeros(KR_LANE - FOX_HEADS), w[:, o_ckv:o_kr],
                                  zeros(LANES - KR_LANE - MLA_ROPE_DIM)], axis=1)
        w_k = _pad_heads(w[:, o_q:o_k], FOX_HEADS, FOX_HEAD_DIM, 0, FOX_HEAD_DIM)
        w_big = jnp.concatenate([w_k, w[:, o_f:o_ckv], w[:, o_kr:], w_misc], axis=1).astype(BF16)
        wqt = w[:, :o_q].T.astype(BF16)
        wvt = w[:, o_k:o_v].T.astype(BF16)
        bias_row = jnp.pad(fox_f_bias[i].astype(F32), (0, LANES - FOX_HEADS)).reshape(1, LANES)
        qk_dim = MLA_NOPE_DIM + MLA_ROPE_DIM
        wuqt = _pad_heads(mla_w_uq[i], MLA_HEADS, qk_dim, 0, qk_dim).T.astype(BF16)
        kv_dim = MLA_NOPE_DIM + MLA_V_DIM
        wuk = _pad_heads(mla_w_ukv[i], MLA_HEADS, kv_dim, 0, MLA_NOPE_DIM).astype(BF16)
        wuv = mla_w_ukv[i].reshape(MLA_KV_RANK, MLA_HEADS, kv_dim)[:, :, MLA_NOPE_DIM:]
        wuvt = wuv.reshape(MLA_KV_RANK, MLA_WIDTH).T.astype(BF16)
        qt_f, k_f, vt_f, g, qt_m, k_m, vt_m = _proj(
            h, wqt, wvt, w_big, pqt, pk, bias_row, tables, row(mla_q_norm[i]), row(mla_kv_norm[i]),
            wuqt, wuk, wuvt, n_real, tpb)
        o_fox = _attention(qt_f, k_f, vt_f, batch, n_real, tpb, "fox_attn")
        o_mla = _attention(qt_m, k_m, vt_m, batch, n_real, tpb, "mla_attn")

        h1, comb_t, gid = _post(o_fox, o_mla, g, h, fox_w_o[i].astype(BF16), mla_w_o[i].astype(BF16),
                                w_out[i].astype(BF16), row(ln1_g[i]), row(ln1_b[i]), rw_hi, rw_lo, rb,
                                alpha)

        wg = jnp.transpose(w_gate[i], (1, 0, 2)).reshape(D_MODEL, N_EXPERTS * D_EXPERT).astype(BF16)
        wu = jnp.transpose(w_up[i], (1, 0, 2)).reshape(D_MODEL, N_EXPERTS * D_EXPERT).astype(BF16)
        wd = w_down[i].reshape(N_EXPERTS * D_EXPERT, D_MODEL).astype(BF16)
        moe_rows = (n_real if i == depth - 1 else n_tiles) * T
        h = _moe(h1, comb_t, gid, wg, wu, wd, row(ln2_g[i]), row(ln2_b[i]), alpha, moe_rows // TM_MOE)

    return h.reshape(batch, seq, D_MODEL)
```

```python
import functools
import math

import jax
import jax.numpy as jnp
from jax import lax
from jax.experimental import pallas as pl
from jax.experimental.pallas import tpu as pltpu

F32 = jnp.float32
BF16 = jnp.bfloat16

D_MODEL = 1024
N_META = 16
BLOCK = 128
NEG_INF = -1e30
LOG2E = math.log2(math.e)

FOX_HEADS = 8
FOX_HEAD_DIM = 64
FOX_WIDTH = FOX_HEADS * FOX_HEAD_DIM

MLA_HEADS = 8
MLA_NOPE_DIM = 64
MLA_ROPE_DIM = 32
MLA_V_DIM = 64
MLA_Q_RANK = 384
MLA_KV_RANK = 256
MLA_WIDTH = MLA_HEADS * MLA_V_DIM
ROPE_THETA = 10000.0

N_EXPERTS = 16
N_GROUPS = 4
EXPERTS_PER_GROUP = N_EXPERTS // N_GROUPS
D_EXPERT = 256

LN_EPS = 1e-5
RMS_EPS = 1e-6

LANES = 128
N_HEADS = FOX_HEADS
ATTN_HEADS = 4
WIDE = N_HEADS * LANES
KR_LANE = 64

E_HI, E_MID, E_LO, E_ONE, E_PAD = 0, N_HEADS, 2 * N_HEADS, 3 * N_HEADS, 3 * N_HEADS + 1
X_CQ, X_ONE_K, X_BIAS = FOX_HEAD_DIM, FOX_HEAD_DIM + 3, FOX_HEAD_DIM + 6
M_BIAS = MLA_NOPE_DIM + MLA_ROPE_DIM

C_K = 0
C_CQ = WIDE
C_CKV = C_CQ + MLA_Q_RANK
C_G = C_CKV + MLA_KV_RANK
C_MISC = C_G + 2 * D_MODEL
PROJ_COLS = C_MISC + LANES

VMEM_LIMIT = 56 * 1024 * 1024

T = 512
META_PER_TILE = T // BLOCK
TM_POST = 512
TM_MOE = 1024
MOE_CHUNK_LOG2 = 7
MOE_CHUNK = 1 << MOE_CHUNK_LOG2

NT_DIMS = (((1,), (1,)), ((), ()))


def _cparams(sem):
    return pltpu.CompilerParams(dimension_semantics=sem, vmem_limit_bytes=VMEM_LIMIT)


def _layer_norm(x, g, b):
    mu = jnp.mean(x, axis=-1, keepdims=True)
    xc = x - mu
    var = jnp.mean(xc * xc, axis=-1, keepdims=True)
    return xc * lax.rsqrt(var + LN_EPS) * g + b


def _rms_norm(x, g):
    ms = jnp.mean(x * x, axis=-1, keepdims=True)
    return x * lax.rsqrt(ms + RMS_EPS) * g


def _ln_in_kernel(x_ref, meta_ref, g_ref, b_ref, o_ref, *, n_real):
    i = pl.program_id(0)

    @pl.when(i < n_real)
    def _():
        o_ref[...] = _layer_norm(x_ref[...], g_ref[...], b_ref[...])

    @pl.when(i >= n_real)
    def _():
        o_ref[...] = jnp.zeros_like(o_ref)
        m = _layer_norm(meta_ref[...], g_ref[...], b_ref[...])
        for jb in range(META_PER_TILE):
            o_ref[jb * BLOCK:jb * BLOCK + N_META, :] = m


def _ln_in(x2d, meta, g, b, n_real, n_tiles):
    full = lambda i: (0, 0)
    return pl.pallas_call(
        functools.partial(_ln_in_kernel, n_real=n_real),
        out_shape=jax.ShapeDtypeStruct((n_tiles * T, D_MODEL), F32),
        grid=(n_tiles,),
        in_specs=[
            pl.BlockSpec((T, D_MODEL), lambda i: (jnp.minimum(i, n_real - 1), 0)),
            pl.BlockSpec((N_META, D_MODEL), full),
            pl.BlockSpec((1, D_MODEL), full),
            pl.BlockSpec((1, D_MODEL), full),
        ],
        out_specs=pl.BlockSpec((T, D_MODEL), lambda i: (i, 0)),
        compiler_params=_cparams(("parallel",)),
        name="ln_in",
    )(x2d, meta, g, b)


def _proj_kernel(x_ref, wqt_ref, wvt_ref, w_ref, pqt_ref, pk_ref, bias_ref,
                 cos_ref, sin_ref, cost_ref, sint_ref, qn_ref, kvn_ref, wuqt_ref, wuk_ref, wuvt_ref,
                 qt_ref, k_ref, vt_ref, g_ref, qtm_ref, km_ref, vtm_ref, carry_ref, c_ref,
                 *, n_real, tpb):
    i = pl.program_id(0)
    x = x_ref[...].astype(BF16)

    def mm(lo, hi):
        return jnp.dot(x, w_ref[:, lo:hi], preferred_element_type=F32)

    misc = mm(C_MISC, PROJ_COLS)
    g_ref[:, :D_MODEL] = mm(C_G, C_G + D_MODEL).astype(BF16)
    g_ref[:, D_MODEL:] = mm(C_G + D_MODEL, C_MISC).astype(BF16)

    z = misc + bias_ref[...]
    logf = jnp.minimum(z, 0.0) - jnp.log1p(jnp.exp(-jnp.abs(z)))
    row = lax.broadcasted_iota(jnp.int32, (T, LANES), 0)
    lane = lax.broadcasted_iota(jnp.int32, (T, LANES), 1)
    blk_row = row % BLOCK
    is_meta = i >= n_real

    @pl.when(jnp.logical_not(is_meta))
    def _():
        @pl.when(i % tpb == 0)
        def _():
            carry_ref[...] = jnp.zeros_like(carry_ref)

        c = logf
        shift = 1
        while shift < T:
            c = c + jnp.where(row >= shift, pltpu.roll(c, shift, 0), 0.0)
            shift *= 2
        c = c + carry_ref[...]
        carry_ref[...] = c[T - 1:T, :]
        c_ref[...] = c

    @pl.when(is_meta)
    def _():
        own = jnp.where(blk_row < N_META, logf, 0.0)
        s = own
        shift = 1
        while shift < BLOCK:
            s = s + jnp.where(blk_row < BLOCK - shift, pltpu.roll(s, T - shift, 0), 0.0)
            shift *= 2
        c_ref[...] = own - s

    is_pad = is_meta & (blk_row >= N_META)
    _mla_operands(mm(C_CQ, C_CKV), mm(C_CKV, C_G), misc, is_pad, cos_ref[...], sin_ref[...],
                  cost_ref[...], sint_ref[...], qn_ref[...], kvn_ref[...], wuqt_ref[...],
                  wuk_ref[...], wuvt_ref[...], qtm_ref, km_ref, vtm_ref)
    c2 = c_ref[...] * LOG2E
    hi = c2.astype(BF16).astype(F32)
    r1 = c2 - hi
    mid = r1.astype(BF16).astype(F32)
    lo = (r1 - mid).astype(BF16).astype(F32)
    feat = jnp.where(lane < E_MID, hi,
           jnp.where(lane < E_LO, pltpu.roll(mid, E_MID, 1),
           jnp.where(lane < E_ONE, pltpu.roll(lo, E_LO, 1),
           jnp.where(lane == E_ONE, 1.0,
           jnp.where((lane == E_PAD) & is_pad, 1.0, 0.0)))))
    feat = feat.astype(BF16)
    extra_k = jnp.dot(feat, pk_ref[...], preferred_element_type=F32)
    extra_qt = lax.dot_general(pqt_ref[...], feat, NT_DIMS, preferred_element_type=F32)

    k_ref[0] = (mm(C_K, C_CQ) + extra_k).astype(BF16)
    qt = lax.dot_general(wqt_ref[...], x, NT_DIMS, preferred_element_type=F32)
    qt = qt * (FOX_HEAD_DIM ** -0.5 * LOG2E)
    for h in range(N_HEADS):
        qt_ref[0, h * LANES:h * LANES + FOX_HEAD_DIM, :] = (
            qt[h * FOX_HEAD_DIM:(h + 1) * FOX_HEAD_DIM, :].astype(BF16))
        qt_ref[0, h * LANES + FOX_HEAD_DIM:(h + 1) * LANES, :] = (
            extra_qt[h * LANES + FOX_HEAD_DIM:(h + 1) * LANES, :].astype(BF16))
    vt_ref[0] = lax.dot_general(wvt_ref[...], x, NT_DIMS, preferred_element_type=F32).astype(BF16)


def _proj(h, wqt, wvt, w_big, pqt, pk, bias_row, tables, qn, kvn, wuqt, wuk, wuvt, n_real, tpb):
    rows = h.shape[0]
    nt = rows // T
    cos_t, sin_t, cos_tt, sin_tt = tables
    half = MLA_ROPE_DIM // 2
    row = lambda i: (i, 0)
    full = lambda i: (0, 0)
    blk = lambda i: (i, 0, 0)
    tab = lambda i: jnp.where(i < n_real, i % tpb, tpb)
    operands = (
        jax.ShapeDtypeStruct((nt, WIDE, T), BF16),
        jax.ShapeDtypeStruct((nt, T, WIDE), BF16),
        jax.ShapeDtypeStruct((nt, N_HEADS * MLA_V_DIM, T), BF16),
    )
    operand_specs = (
        pl.BlockSpec((1, WIDE, T), blk),
        pl.BlockSpec((1, T, WIDE), blk),
        pl.BlockSpec((1, N_HEADS * MLA_V_DIM, T), blk),
    )
    return pl.pallas_call(
        functools.partial(_proj_kernel, n_real=n_real, tpb=tpb),
        out_shape=operands + (jax.ShapeDtypeStruct((rows, 2 * D_MODEL), BF16),) + operands,
        grid=(nt,),
        in_specs=[
            pl.BlockSpec((T, D_MODEL), row),
            pl.BlockSpec((FOX_WIDTH, D_MODEL), full),
            pl.BlockSpec((FOX_WIDTH, D_MODEL), full),
            pl.BlockSpec((D_MODEL, PROJ_COLS), full),
            pl.BlockSpec((WIDE, LANES), full),
            pl.BlockSpec((LANES, WIDE), full),
            pl.BlockSpec((1, LANES), full),
            pl.BlockSpec((T, LANES), lambda i: (tab(i), 0)),
            pl.BlockSpec((T, LANES), lambda i: (tab(i), 0)),
            pl.BlockSpec((half, T), lambda i: (0, tab(i))),
            pl.BlockSpec((half, T), lambda i: (0, tab(i))),
            pl.BlockSpec((1, MLA_Q_RANK), full),
            pl.BlockSpec((1, MLA_KV_RANK), full),
            pl.BlockSpec((WIDE, MLA_Q_RANK), full),
            pl.BlockSpec((MLA_KV_RANK, WIDE), full),
            pl.BlockSpec((MLA_WIDTH, MLA_KV_RANK), full),
        ],
        out_specs=operand_specs + (pl.BlockSpec((T, 2 * D_MODEL), row),) + operand_specs,
        scratch_shapes=[pltpu.VMEM((1, LANES), F32), pltpu.VMEM((T, LANES), F32)],
        compiler_params=_cparams(("arbitrary",)),
        name="in_proj",
    )(h, wqt, wvt, w_big, pqt, pk, bias_row, cos_t, sin_t, cos_tt, sin_tt, qn, kvn, wuqt, wuk, wuvt)


def _decay_placement():
    pk = [[0.0] * WIDE for _ in range(LANES)]
    pqt = [[0.0] * LANES for _ in range(WIDE)]
    for h in range(N_HEADS):
        base = h * LANES
        for s, e in enumerate((E_HI, E_MID, E_LO)):
            pqt[base + X_CQ + s][e + h] = 1.0
            pk[E_ONE][base + X_CQ + s] = 1.0
            pqt[base + X_ONE_K + s][E_ONE] = 1.0
            pk[e + h][base + X_ONE_K + s] = -1.0
        pqt[base + X_BIAS][E_ONE] = 1.0
        pk[E_PAD][base + X_BIAS] = NEG_INF
    return jnp.array(pqt, F32).astype(BF16), jnp.array(pk, F32).astype(BF16)


def _attn_kernel(qt_ref, k_ref, vt_ref, qtm_ref, km_ref, vtm_ref, o_ref, om_ref, st_ref, stm_ref):
    nh = ATTN_HEADS
    n_blocks = qt_ref.shape[0]
    km = km_ref[0]
    vtm = vtm_ref[0]

    def causal(n):
        return (lax.broadcasted_iota(jnp.int32, (n, n), 0)
                <= lax.broadcasted_iota(jnp.int32, (n, n), 1))

    def head(a, jj, width):
        return a[jj * width:(jj + 1) * width]

    qtm = qtm_ref[0]
    outs = []
    for jj in range(nh):
        st = jnp.dot(km[:, jj * LANES:(jj + 1) * LANES], head(qtm, jj, LANES),
                     preferred_element_type=F32)
        st = jnp.where(causal(BLOCK), st, NEG_INF)
        p = jnp.exp2(st - jnp.max(st, axis=0, keepdims=True))
        pv = jnp.dot(head(vtm, jj, MLA_V_DIM), p.astype(BF16), preferred_element_type=F32)
        outs.append(pv / jnp.sum(p, axis=0, keepdims=True))
    om_ref[...] = jnp.concatenate(outs, axis=0).T.astype(om_ref.dtype)

    def query_block(qi, carry):
        qt = qt_ref[qi]

        def scores(kj, slot):
            k = k_ref[kj]
            for jj in range(nh):
                st_ref[slot, jj] = jnp.dot(k[:, jj * LANES:(jj + 1) * LANES], head(qt, jj, LANES),
                                           preferred_element_type=F32)

        def consume(kj, slot, state, diagonal):
            vt = vt_ref[kj]
            out = []
            for jj in range(nh):
                m, l, acc = state[jj]
                st = st_ref[slot, jj]
                m_new = m
                if diagonal:
                    st = jnp.where(causal(T), st, NEG_INF)
                    stm = stm_ref[jj]
                    m_new = jnp.maximum(m_new, jnp.max(stm, axis=0, keepdims=True))
                m_new = jnp.maximum(m_new, jnp.max(st, axis=0, keepdims=True))
                alpha = jnp.exp2(m - m_new)
                p = jnp.exp2(st - m_new)
                l_new = alpha * l + jnp.sum(p, axis=0, keepdims=True)
                pv = jnp.dot(head(vt, jj, MLA_V_DIM), p.astype(BF16), preferred_element_type=F32)
                if diagonal:
                    pm = jnp.exp2(stm - m_new)
                    l_new = l_new + jnp.sum(pm, axis=0, keepdims=True)
                    pv = pv + jnp.dot(head(vtm, jj, MLA_V_DIM), pm.astype(BF16),
                                      preferred_element_type=F32)
                out.append((m_new, l_new, alpha * acc + pv))
            return tuple(out)

        init_one = (jnp.full((1, T), NEG_INF, F32), jnp.zeros((1, T), F32),
                    jnp.zeros((MLA_V_DIM, T), F32))
        state = (init_one,) * nh

        def pair(i, state):
            c0 = 2 * i
            scores(c0 + 1, 1)
            state = consume(c0, 0, state, False)
            scores(c0 + 2, 0)
            return consume(c0 + 1, 1, state, False)

        def odd_tail(state):
            scores(qi, 1)
            state = consume(qi - 1, 0, state, False)
            return consume(qi, 1, state, True)

        def even_tail(state):
            return consume(qi, 0, state, True)

        scores(0, 0)
        for jj in range(nh):
            stm_ref[jj] = jnp.dot(km[:, jj * LANES:(jj + 1) * LANES], head(qt, jj, LANES),
                                  preferred_element_type=F32)
        state = lax.fori_loop(0, qi // 2, pair, state)
        state = lax.cond(qi % 2 == 1, odd_tail, even_tail, state)
        ot = jnp.concatenate([acc / l for (_, l, acc) in state], axis=0)
        o_ref[pl.ds(pl.multiple_of(qi * T, T), T), :] = ot.T.astype(o_ref.dtype)
        return carry

    lax.fori_loop(0, n_blocks, query_block, 0)


def _attention(qt, k, vt, batch, n_real, tpb, name):
    nh = ATTN_HEADS
    qk_w, v_w = nh * LANES, nh * MLA_V_DIM
    meta_tile = lambda b: n_real + b // META_PER_TILE
    meta_blk = lambda b: b % META_PER_TILE
    return pl.pallas_call(
        _attn_kernel,
        out_shape=(
            jax.ShapeDtypeStruct((n_real * T, N_HEADS * MLA_V_DIM), BF16),
            jax.ShapeDtypeStruct((batch * BLOCK, N_HEADS * MLA_V_DIM), BF16),
        ),
        grid=(batch, N_HEADS // nh),
        in_specs=[
            pl.BlockSpec((tpb, qk_w, T), lambda b, hg: (b, hg, 0)),
            pl.BlockSpec((tpb, T, qk_w), lambda b, hg: (b, 0, hg)),
            pl.BlockSpec((tpb, v_w, T), lambda b, hg: (b, hg, 0)),
            pl.BlockSpec((1, qk_w, BLOCK), lambda b, hg: (meta_tile(b), hg, meta_blk(b))),
            pl.BlockSpec((1, BLOCK, qk_w), lambda b, hg: (meta_tile(b), meta_blk(b), hg)),
            pl.BlockSpec((1, v_w, BLOCK), lambda b, hg: (meta_tile(b), hg, meta_blk(b))),
        ],
        out_specs=(
            pl.BlockSpec((tpb * T, v_w), lambda b, hg: (b, hg)),
            pl.BlockSpec((BLOCK, v_w), lambda b, hg: (b, hg)),
        ),
        scratch_shapes=[pltpu.VMEM((2, nh, T, T), F32),
                        pltpu.VMEM((nh, BLOCK, T), F32)],
        compiler_params=_cparams(("parallel", "parallel")),
        name=name,
    )(qt, k, vt, qt, k, vt)


def _mla_operands(cq, ckv, misc, is_pad, cos, sin, cost, sint, qn_g, kvn_g, wuqt, wuk, wuvt,
                  qt_ref, k_ref, vt_ref):
    half = MLA_ROPE_DIM // 2

    qn = _rms_norm(cq, qn_g).astype(BF16)
    qt = lax.dot_general(wuqt, qn, NT_DIMS, preferred_element_type=F32)
    qt = qt * ((MLA_NOPE_DIM + MLA_ROPE_DIM) ** -0.5 * LOG2E)
    tail =jnp.where(lax.broadcasted_iota(jnp.int32, (LANES - M_BIAS, T), 0) == 0, 1.0, 0.0)
    for h in range(N_HEADS):
        base = h * LANES
        x1 = qt[base + MLA_NOPE_DIM:base + MLA_NOPE_DIM + half, :]
        x2 = qt[base + MLA_NOPE_DIM + half:base + M_BIAS, :]
        qt_ref[0, base:base + MLA_NOPE_DIM, :] = qt[base:base + MLA_NOPE_DIM, :].astype(BF16)
        qt_ref[0, base + MLA_NOPE_DIM:base + MLA_NOPE_DIM + half, :] = (x1 * cost - x2 * sint).astype(BF16)
        qt_ref[0, base + MLA_NOPE_DIM + half:base + M_BIAS, :] = (x2 * cost + x1 * sint).astype(BF16)
        qt_ref[0, base + M_BIAS:base + LANES, :] = tail.astype(BF16)

    kvn = _rms_norm(ckv, kvn_g).astype(BF16)
    k_nope = jnp.dot(kvn, wuk, preferred_element_type=F32)
    lane = lax.broadcasted_iota(jnp.int32, (T, LANES), 1)
    kr = jnp.where((lane >= KR_LANE) & (lane < KR_LANE + MLA_ROPE_DIM), misc, 0.0)
    from_hi = pltpu.roll(kr, LANES - half, 1)
    from_lo = pltpu.roll(kr, half, 1)
    swapped = jnp.where(lane < KR_LANE + half, -from_hi, from_lo)
    k_rot = kr * cos + swapped * sin
    k_rot = jnp.where((lane == M_BIAS) & is_pad, NEG_INF, k_rot)
    k_ref[0] = (k_nope + jnp.concatenate([k_rot] * N_HEADS, axis=1)).astype(BF16)
    vt_ref[0] = lax.dot_general(wuvt, kvn, NT_DIMS, preferred_element_type=F32).astype(BF16)


def _top2_sum(a, b, c, d):
    hi1, lo1 = jnp.maximum(a, b), jnp.minimum(a, b)
    hi2, lo2 = jnp.maximum(c, d), jnp.minimum(c, d)
    return jnp.maximum(hi1, hi2) + jnp.maximum(jnp.minimum(hi1, hi2), jnp.maximum(lo1, lo2))


def _route(logits_t, bias_col):
    scores = jax.nn.sigmoid(logits_t)
    biased = scores + bias_col
    b = [biased[e:e + 1, :] for e in range(N_EXPERTS)]
    s = [scores[e:e + 1, :] for e in range(N_EXPERTS)]
    gscore = [_top2_sum(*b[EXPERTS_PER_GROUP * g:EXPERTS_PER_GROUP * (g + 1)]) for g in range(N_GROUPS)]
    best = gscore[0]
    gidx = jnp.zeros_like(best, dtype=jnp.int32)
    for g in range(1, N_GROUPS):
        better = gscore[g] > best
        gidx = jnp.where(better, g, gidx)
        best = jnp.where(better, gscore[g], best)
    in_g = [gidx == g for g in range(N_GROUPS)]

    def pick(vals, j):
        out = vals[j]
        for g in range(1, N_GROUPS):
            out = jnp.where(in_g[g], vals[EXPERTS_PER_GROUP * g + j], out)
        return out

    vb = [pick(b, j) for j in range(EXPERTS_PER_GROUP)]
    vs = [pick(s, j) for j in range(EXPERTS_PER_GROUP)]
    chosen = []
    for j in range(EXPERTS_PER_GROUP):
        rank = jnp.zeros_like(gidx)
        for i in range(EXPERTS_PER_GROUP):
            if i == j:
                continue
            ahead = (vb[i] >= vb[j]) if i < j else (vb[i] > vb[j])
            rank = rank + jnp.where(ahead, 1, 0)
        chosen.append(rank < 2)
    total = sum(jnp.where(chosen[j], vs[j], 0.0) for j in range(EXPERTS_PER_GROUP))
    gates = [jnp.where(chosen[j], vs[j] / total, 0.0) for j in range(EXPERTS_PER_GROUP)]
    rows = []
    for g in range(N_GROUPS):
        for j in range(EXPERTS_PER_GROUP):
            rows.append(jnp.where(in_g[g], gates[j], 0.0))
    return jnp.concatenate(rows, axis=0), gidx


def _post_kernel(of_ref, ofm_ref, om_ref, omm_ref, g_ref, h_ref, wfo_ref, wmo_ref, wout_ref,
                 lng_ref, lnb_ref, rwh_ref, rwl_ref, rb_ref, h1_ref, comb_ref, gid_ref,
                 *, alpha, n_real_tiles):
    is_meta = pl.program_id(0) >= n_real_tiles
    o_fox = jnp.where(is_meta, ofm_ref[...], of_ref[...])
    o_mla = jnp.where(is_meta, omm_ref[...], om_ref[...])
    y_fox = jnp.dot(o_fox, wfo_ref[...], preferred_element_type=F32)
    y_mla = jnp.dot(o_mla, wmo_ref[...], preferred_element_type=F32)
    merged = (jax.nn.sigmoid(g_ref[:, :D_MODEL].astype(F32)) * y_fox
              + jax.nn.sigmoid(g_ref[:, D_MODEL:].astype(F32)) * y_mla)
    mix = jnp.dot(merged.astype(BF16), wout_ref[...], preferred_element_type=F32)
    h1 = _layer_norm(alpha * h_ref[...] + mix, lng_ref[...], lnb_ref[...])
    h1_ref[...] = h1
    h_hi = h1.astype(BF16)
    h_lo = (h1 - h_hi.astype(F32)).astype(BF16)
    rwh, rwl = rwh_ref[...], rwl_ref[...]
    logits_t = (lax.dot_general(rwh, h_hi, NT_DIMS, preferred_element_type=F32)
                + lax.dot_general(rwl, h_hi, NT_DIMS, preferred_element_type=F32)
                + lax.dot_general(rwh, h_lo, NT_DIMS, preferred_element_type=F32))
    comb_ref[...], gid_ref[...] = _route(logits_t, rb_ref[...])


def _post(o_fox, o_mla, g, h, wfo, wmo, wout, lng, lnb, rwh, rwl, rb, alpha):
    rows = h.shape[0]
    tm = TM_POST
    o_fox_r, o_fox_m = o_fox
    o_mla_r, o_mla_m = o_mla
    nrt = o_fox_r.shape[0] // tm
    row = lambda i: (i, 0)
    real = lambda i: (jnp.minimum(i, nrt - 1), 0)
    meta = lambda i: (jnp.maximum(i - nrt, 0), 0)
    full = lambda i: (0, 0)
    return pl.pallas_call(
        functools.partial(_post_kernel, alpha=alpha, n_real_tiles=nrt),
        out_shape=(
            jax.ShapeDtypeStruct((rows, D_MODEL), F32),
            jax.ShapeDtypeStruct((N_EXPERTS, rows), F32),
            jax.ShapeDtypeStruct((1, rows), jnp.int32),
        ),
        grid=(rows // tm,),
        in_specs=[
            pl.BlockSpec((tm, FOX_WIDTH), real),
            pl.BlockSpec((tm, FOX_WIDTH), meta),
            pl.BlockSpec((tm, MLA_WIDTH), real),
            pl.BlockSpec((tm, MLA_WIDTH), meta),
            pl.BlockSpec((tm, 2 * D_MODEL), row),
            pl.BlockSpec((tm, D_MODEL), row),
            pl.BlockSpec((FOX_WIDTH, D_MODEL), full),
            pl.BlockSpec((MLA_WIDTH, D_MODEL), full),
            pl.BlockSpec((D_MODEL, D_MODEL), full),
            pl.BlockSpec((1, D_MODEL), full),
            pl.BlockSpec((1, D_MODEL), full),
            pl.BlockSpec((N_EXPERTS, D_MODEL), full),
            pl.BlockSpec((N_EXPERTS, D_MODEL), full),
            pl.BlockSpec((N_EXPERTS, 1), full),
        ],
        out_specs=(
            pl.BlockSpec((tm, D_MODEL), row),
            pl.BlockSpec((N_EXPERTS, tm), lambda i: (0, i)),
            pl.BlockSpec((1, tm), lambda i: (0, i)),
        ),
        compiler_params=_cparams(("parallel",)),
        name="merge_ln1_router",
    )(o_fox_r, o_fox_m, o_mla_r, o_mla_m, g, h, wfo, wmo, wout, lng, lnb, rwh, rwl, rb)


def _moe_kernel(cnt_ref, h_ref, comb_ref, gid_ref, wg_ref, wu_ref, wd_ref, lng_ref, lnb_ref, o_ref,
                p_ref, pt_ref, xs_ref, cws_ref, ys_ref, *, alpha):
    i = pl.program_id(0)
    g = pl.program_id(1)
    tm, slots = h_ref.shape[0], p_ref.shape[0]

    def chunks(gg):
        return lax.shift_right_logical(cnt_ref[i * N_GROUPS + gg] + (MOE_CHUNK - 1), MOE_CHUNK_LOG2)

    @pl.when(g == 0)
    def _():
        gid = gid_ref[...]
        sub = lax.broadcasted_iota(jnp.int32, (8, tm), 0)
        lane = lax.broadcasted_iota(jnp.int32, (8, tm), 1)
        onehot = jnp.where(gid == sub, 1, 0)
        cum = onehot
        shift = 1
        while shift < tm:
            cum = cum + jnp.where(lane >= shift, pltpu.roll(cum, shift, 1), 0)
            shift *= 2
        slot = jnp.zeros((1, tm), jnp.int32)
        off = jnp.int32(0)
        for gg in range(N_GROUPS):
            slot = slot + onehot[gg:gg + 1, :] * (cum[gg:gg + 1, :] - 1 + off)
            off = off + chunks(gg) * MOE_CHUNK
        srow = lax.broadcasted_iota(jnp.int32, (slots, tm), 0)
        p = jnp.where(srow == slot, 1.0, 0.0).astype(BF16)
        p_ref[...] = p
        stack = jnp.concatenate([comb_ref[...], slot.astype(F32),
                                 jnp.zeros((LANES - N_EXPERTS - 1, tm), F32)], axis=0)
        nat = stack.T
        lane_n = lax.broadcasted_iota(jnp.int32, (tm, LANES), 1)
        cw = jnp.where(lane_n < N_EXPERTS, nat, 0.0)
        slot_col = jnp.sum(jnp.where(lane_n == N_EXPERTS, nat, 0.0), axis=1, keepdims=True)
        scol = lax.broadcasted_iota(jnp.int32, (tm, slots), 1)
        pt_ref[...] = jnp.where(scol == slot_col.astype(jnp.int32), 1.0, 0.0).astype(BF16)
        xs_ref[...] = jnp.dot(p, h_ref[...].astype(BF16), preferred_element_type=F32).astype(BF16)
        cw_hi = cw.astype(BF16)
        cw_lo = (cw - cw_hi.astype(F32)).astype(BF16)
        both = jnp.dot(p, jnp.concatenate([cw_hi, cw_lo], axis=1), preferred_element_type=F32)
        cws_ref[...] = both[:, :LANES] + both[:, LANES:]
        ys_ref[...] = jnp.zeros_like(ys_ref)

    off_g = jnp.int32(0)
    for gg in range(N_GROUPS - 1):
        off_g = off_g + jnp.where(gg < g, chunks(gg), 0) * MOE_CHUNK
    lane_c = lax.broadcasted_iota(jnp.int32, (MOE_CHUNK, LANES), 1)

    def chunk(k, carry):
        r0 = pl.multiple_of(off_g + k * MOE_CHUNK, MOE_CHUNK)
        x = xs_ref[pl.ds(r0, MOE_CHUNK), :]
        gate = jnp.dot(x, wg_ref[...], preferred_element_type=F32)
        up = jnp.dot(x, wu_ref[...], preferred_element_type=F32)
        hid = gate * jax.nn.sigmoid(gate) * up
        cw = cws_ref[pl.ds(r0, MOE_CHUNK), :]
        pieces = []
        for e in range(EXPERTS_PER_GROUP):
            w_e = jnp.sum(jnp.where(lane_c == g * EXPERTS_PER_GROUP + e, cw, 0.0), axis=1, keepdims=True)
            pieces.append((hid[:, e * D_EXPERT:(e + 1) * D_EXPERT] * w_e).astype(BF16))
        y = jnp.dot(jnp.concatenate(pieces, axis=1), wd_ref[...], preferred_element_type=F32)
        ys_ref[pl.ds(r0, MOE_CHUNK), :] = y.astype(BF16)
        return carry

    lax.fori_loop(0, chunks(g), chunk, 0)

    @pl.when(g == N_GROUPS - 1)
    def _():
        y = jnp.dot(pt_ref[...], ys_ref[...], preferred_element_type=F32)
        o_ref[...] = _layer_norm(alpha * h_ref[...] + y, lng_ref[...], lnb_ref[...])


def _moe(h1, comb_t, gid, wg, wu, wd, lng, lnb, alpha, n_tiles):
    tm = TM_MOE
    slots = tm + N_GROUPS * MOE_CHUNK
    hidden = EXPERTS_PER_GROUP * D_EXPERT
    tile_gid = gid[0, :n_tiles * tm].reshape(n_tiles, tm)
    cnt = jnp.sum(tile_gid[:, :, None] == jnp.arange(N_GROUPS, dtype=jnp.int32), axis=1,
                  dtype=jnp.int32).reshape(-1)
    return pl.pallas_call(
        functools.partial(_moe_kernel, alpha=alpha),
        out_shape=jax.ShapeDtypeStruct((n_tiles * tm, D_MODEL), F32),
        grid_spec=pltpu.PrefetchScalarGridSpec(
            num_scalar_prefetch=1,
            grid=(n_tiles, N_GROUPS),
            in_specs=[
                pl.BlockSpec((tm, D_MODEL), lambda i, g, cnt: (i, 0)),
                pl.BlockSpec((N_EXPERTS, tm), lambda i, g, cnt: (0, i)),
                pl.BlockSpec((1, tm), lambda i, g, cnt: (0, i)),
                pl.BlockSpec((D_MODEL, hidden), lambda i, g, cnt: (0, g)),
                pl.BlockSpec((D_MODEL, hidden), lambda i, g, cnt: (0, g)),
                pl.BlockSpec((hidden, D_MODEL), lambda i, g, cnt: (g, 0)),
                pl.BlockSpec((1, D_MODEL), lambda i, g, cnt: (0, 0)),
                pl.BlockSpec((1, D_MODEL), lambda i, g, cnt: (0, 0)),
            ],
            out_specs=pl.BlockSpec((tm, D_MODEL), lambda i, g, cnt: (i, 0)),
            scratch_shapes=[
                pltpu.VMEM((slots, tm), BF16),
                pltpu.VMEM((tm, slots), BF16),
                pltpu.VMEM((slots, D_MODEL), BF16),
                pltpu.VMEM((slots, LANES), F32),
                pltpu.VMEM((slots, D_MODEL), BF16),
            ],
        ),
        compiler_params=_cparams(("parallel", "arbitrary")),
        name="moe_ln2",
    )(cnt, h1, comb_t, gid, wg, wu, wd, lng, lnb)


def _rope_tables(seq):
    half = MLA_ROPE_DIM // 2
    blk = jnp.arange(T, dtype=jnp.int32) % BLOCK
    pos = jnp.concatenate([jnp.arange(seq, dtype=jnp.int32) + N_META,
                           jnp.where(blk < N_META, blk, 0)]).astype(F32)
    rows = seq + T
    inv = ROPE_THETA ** (-jnp.arange(half, dtype=F32) / half)
    ang = pos[:, None] * inv[None, :]
    cos, sin = jnp.cos(ang), jnp.sin(ang)
    ones = jnp.ones((rows, KR_LANE), F32)
    tail = LANES - KR_LANE - MLA_ROPE_DIM
    cos_t = jnp.concatenate([ones, cos, cos, jnp.ones((rows, tail), F32)], axis=1)
    sin_t = jnp.concatenate([0 * ones, sin, sin, jnp.zeros((rows, tail), F32)], axis=1)
    return cos_t, sin_t, cos.T, sin.T


def _pad_heads(w, n_heads, per_head, keep_lo, keep_hi):
    k = w.shape[0]
    w = w.reshape(k, n_heads, per_head)[:, :, keep_lo:keep_hi]
    w = jnp.pad(w, ((0, 0), (0, 0), (0, LANES - (keep_hi - keep_lo))))
    return w.reshape(k, n_heads * LANES)


def kernel(x, meta_tokens, ln_in_g, ln_in_b, w_in, fox_f_bias, fox_w_o, mla_q_norm, mla_w_uq,
           mla_kv_norm, mla_w_ukv, mla_w_o, w_out, ln1_g, ln1_b, router_w, router_b,
           w_gate, w_up, w_down, ln2_g, ln2_b):
    batch, seq, _ = x.shape
    depth = w_in.shape[0]
    assert seq % T == 0 and batch % META_PER_TILE == 0
    tpb = seq // T
    n_real = batch * tpb
    n_tiles = n_real + batch // META_PER_TILE
    assert (n_real * T) % TM_MOE == 0 and (n_tiles * T) % TM_MOE == 0
    alpha = (2 * depth) ** 0.25
    row = lambda a: a.reshape(1, -1).astype(F32)

    h = _ln_in(x.reshape(batch * seq, D_MODEL), meta_tokens.astype(F32), row(ln_in_g), row(ln_in_b),
               n_real, n_tiles)
    tables = _rope_tables(seq)
    pqt, pk = _decay_placement()
    rw_t = router_w.T.astype(F32)
    rw_hi = rw_t.astype(BF16)
    rw_lo = (rw_t - rw_hi.astype(F32)).astype(BF16)
    rb = router_b.reshape(N_EXPERTS, 1).astype(F32)

    o_q = FOX_WIDTH
    o_k = o_q + FOX_WIDTH
    o_v = o_k + FOX_WIDTH
    o_f = o_v + FOX_HEADS
    o_cq = o_f + MLA_Q_RANK
    o_ckv = o_cq + MLA_KV_RANK
    o_kr = o_ckv + MLA_ROPE_DIM
    for i in range(depth):
        w = w_in[i]
        zeros = lambda n: jnp.zeros((D_MODEL, n), w.dtype)
        w_misc = jnp.concatenate([w[:, o_v:o_f], zeros(KR_LANE - FOX_HEADS), w[:, o_ckv:o_kr],
                                  zeros(LANES - KR_LANE - MLA_ROPE_DIM)], axis=1)
        w_k = _pad_heads(w[:, o_q:o_k], FOX_HEADS, FOX_HEAD_DIM, 0, FOX_HEAD_DIM)
        w_big = jnp.concatenate([w_k, w[:, o_f:o_ckv], w[:, o_kr:], w_misc], axis=1).astype(BF16)
        wqt = w[:, :o_q].T.astype(BF16)
        wvt = w[:, o_k:o_v].T.astype(BF16)
        bias_row = jnp.pad(fox_f_bias[i].astype(F32), (0, LANES - FOX_HEADS)).reshape(1, LANES)
        qk_dim = MLA_NOPE_DIM + MLA_ROPE_DIM
        wuqt = _pad_heads(mla_w_uq[i], MLA_HEADS, qk_dim, 0, qk_dim).T.astype(BF16)
        kv_dim = MLA_NOPE_DIM + MLA_V_DIM
        wuk = _pad_heads(mla_w_ukv[i], MLA_HEADS, kv_dim, 0, MLA_NOPE_DIM).astype(BF16)
        wuv = mla_w_ukv[i].reshape(MLA_KV_RANK, MLA_HEADS, kv_dim)[:, :, MLA_NOPE_DIM:]
        wuvt = wuv.reshape(MLA_KV_RANK, MLA_WIDTH).T.astype(BF16)
        qt_f, k_f, vt_f, g, qt_m, k_m, vt_m = _proj(
            h, wqt, wvt, w_big, pqt, pk, bias_row, tables, row(mla_q_norm[i]), row(mla_kv_norm[i]),
            wuqt, wuk, wuvt, n_real, tpb)
        o_fox = _attention(qt_f, k_f, vt_f, batch, n_real, tpb, "fox_attn")
        o_mla = _attention(qt_m, k_m, vt_m, batch, n_real, tpb, "mla_attn")

        h1, comb_t, gid = _post(o_fox, o_mla, g, h, fox_w_o[i].astype(BF16), mla_w_o[i].astype(BF16),
                                w_out[i].astype(BF16), row(ln1_g[i]), row(ln1_b[i]), rw_hi, rw_lo, rb,
                                alpha)

        wg = jnp.transpose(w_gate[i], (1, 0, 2)).reshape(D_MODEL, N_EXPERTS * D_EXPERT).astype(BF16)
        wu = jnp.transpose(w_up[i], (1, 0, 2)).reshape(D_MODEL, N_EXPERTS * D_EXPERT).astype(BF16)
        wd = w_down[i].reshape(N_EXPERTS * D_EXPERT, D_MODEL).astype(BF16)
        moe_rows = (n_real if i == depth - 1 else n_tiles) * T
        h = _moe(h1, comb_t, gid, wg, wu, wd, row(ln2_g[i]), row(ln2_b[i]), alpha, moe_rows // TM_MOE)

    return h.reshape(batch, seq, D_MODEL)
```

```python
import functools
import math

import jax
import jax.numpy as jnp
import numpy as np
from jax import lax
from jax.experimental import pallas as pl
from jax.experimental.pallas import tpu as pltpu

F32 = jnp.float32
BF16 = jnp.bfloat16

D_MODEL = 1024
N_META = 16
BLOCK = 128
NEG_INF = -1e30
LOG2E = math.log2(math.e)

FOX_HEADS = 8
FOX_HEAD_DIM = 64
FOX_WIDTH = FOX_HEADS * FOX_HEAD_DIM

MLA_HEADS = 8
MLA_NOPE_DIM = 64
MLA_ROPE_DIM = 32
MLA_V_DIM = 64
MLA_Q_RANK = 384
MLA_KV_RANK = 256
MLA_WIDTH = MLA_HEADS * MLA_V_DIM
ROPE_THETA = 10000.0

N_EXPERTS = 16
N_GROUPS = 4
EXPERTS_PER_GROUP = N_EXPERTS // N_GROUPS
D_EXPERT = 256

LN_EPS = 1e-5
RMS_EPS = 1e-6

LANES = 128
N_HEADS = FOX_HEADS
ATTN_HEADS = 4
WIDE = N_HEADS * LANES
KR_LANE = 64

E_HI, E_MID, E_LO, E_ONE, E_PAD = 0, N_HEADS, 2 * N_HEADS, 3 * N_HEADS, 3 * N_HEADS + 1
X_CQ, X_ONE_K, X_BIAS = FOX_HEAD_DIM, FOX_HEAD_DIM + 3, FOX_HEAD_DIM + 6
M_BIAS = MLA_NOPE_DIM + MLA_ROPE_DIM

C_K = 0
C_CQ = WIDE
C_CKV = C_CQ + MLA_Q_RANK
C_G = C_CKV + MLA_KV_RANK
C_MISC = C_G + 2 * D_MODEL
PROJ_COLS = C_MISC + LANES

VMEM_LIMIT = 56 * 1024 * 1024

T = 512
META_PER_TILE = T // BLOCK
TM_POST = 512
TM_MOE = 1024
MOE_CHUNK_LOG2 = 7
MOE_CHUNK = 1 << MOE_CHUNK_LOG2

NT_DIMS = (((1,), (1,)), ((), ()))


def _cparams(sem):
    return pltpu.CompilerParams(dimension_semantics=sem, vmem_limit_bytes=VMEM_LIMIT)


def _layer_norm(x, g, b):
    mu = jnp.mean(x, axis=-1, keepdims=True)
    xc = x - mu
    var = jnp.mean(xc * xc, axis=-1, keepdims=True)
    return xc * lax.rsqrt(var + LN_EPS) * g + b


def _rms_norm(x, g):
    ms = jnp.mean(x * x, axis=-1, keepdims=True)
    return x * lax.rsqrt(ms + RMS_EPS) * g


def _proj_kernel(*refs, n_real, tpb, first):
    if first:
        x_ref, meta_ref, lng_ref, lnb_ref = refs[:4]
        refs = refs[4:]
    else:
        x_ref, refs = refs[0], refs[1:]
    (wqt_ref, wvt_ref, w_ref, pqt_ref, pk_ref, bias_ref, cos_ref, sin_ref, cost_ref, sint_ref,
     qn_ref, kvn_ref, wuqt_ref, wuk_ref, wuvt_ref,
     qt_ref, k_ref, vt_ref, g_ref, qtm_ref, km_ref, vtm_ref) = refs[:22]
    carry_ref, c_ref = refs[-2:]
    i = pl.program_id(0)
    if first:
        h_ref = refs[22]

        @pl.when(i < n_real)
        def _():
            h_ref[...] = _layer_norm(x_ref[...], lng_ref[...], lnb_ref[...])

        @pl.when(i >= n_real)
        def _():
            h_ref[...] = jnp.zeros_like(h_ref)
            m = _layer_norm(meta_ref[...], lng_ref[...], lnb_ref[...])
            for jb in range(META_PER_TILE):
                h_ref[jb * BLOCK:jb * BLOCK + N_META, :] = m

        x = h_ref[...].astype(BF16)
    else:
        x = x_ref[...].astype(BF16)

    def mm(lo, hi):
        return jnp.dot(x, w_ref[:, lo:hi], preferred_element_type=F32)

    misc = mm(C_MISC, PROJ_COLS)
    g_ref[:, :D_MODEL] = mm(C_G, C_G + D_MODEL).astype(BF16)
    g_ref[:, D_MODEL:] = mm(C_G + D_MODEL, C_MISC).astype(BF16)

    z = misc + bias_ref[...]
    logf = jnp.minimum(z, 0.0) - jnp.log1p(jnp.exp(-jnp.abs(z)))
    row = lax.broadcasted_iota(jnp.int32, (T, LANES), 0)
    lane = lax.broadcasted_iota(jnp.int32, (T, LANES), 1)
    blk_row = row % BLOCK
    is_meta = i >= n_real

    @pl.when(jnp.logical_not(is_meta))
    def _():
        @pl.when(i % tpb == 0)
        def _():
            carry_ref[...] = jnp.zeros_like(carry_ref)

        c = logf
        shift = 1
        while shift < T:
            c = c + jnp.where(row >= shift, pltpu.roll(c, shift, 0), 0.0)
            shift *= 2
        c = c + carry_ref[...]
        carry_ref[...] = c[T - 1:T, :]
        c_ref[...] = c

    @pl.when(is_meta)
    def _():
        own = jnp.where(blk_row < N_META, logf, 0.0)
        s = own
        shift = 1
        while shift < BLOCK:
            s = s + jnp.where(blk_row < BLOCK - shift, pltpu.roll(s, T - shift, 0), 0.0)
            shift *= 2
        c_ref[...] = own - s

    is_pad = is_meta & (blk_row >= N_META)
    _mla_operands(mm(C_CQ, C_CKV), mm(C_CKV, C_G), misc, is_pad, cos_ref[...], sin_ref[...],
                  cost_ref[...], sint_ref[...], qn_ref[...], kvn_ref[...], wuqt_ref[...],
                  wuk_ref[...], wuvt_ref[...], qtm_ref, km_ref, vtm_ref)
    c2 = c_ref[...] * LOG2E
    hi = c2.astype(BF16).astype(F32)
    r1 = c2 - hi
    mid = r1.astype(BF16).astype(F32)
    lo = (r1 - mid).astype(BF16).astype(F32)
    feat = jnp.where(lane < E_MID, hi,
           jnp.where(lane < E_LO, pltpu.roll(mid, E_MID, 1),
           jnp.where(lane < E_ONE, pltpu.roll(lo, E_LO, 1),
           jnp.where(lane == E_ONE, 1.0,
           jnp.where((lane == E_PAD) & is_pad, 1.0, 0.0)))))
    feat = feat.astype(BF16)
    extra_k = jnp.dot(feat, pk_ref[...], preferred_element_type=F32)
    extra_qt = lax.dot_general(pqt_ref[...], feat, NT_DIMS, preferred_element_type=F32)

    k_ref[0] = (mm(C_K, C_CQ) + extra_k).astype(BF16)
    qt = lax.dot_general(wqt_ref[...], x, NT_DIMS, preferred_element_type=F32)
    qt = qt * (FOX_HEAD_DIM ** -0.5 * LOG2E)
    for h in range(N_HEADS):
        qt_ref[0, h * LANES:h * LANES + FOX_HEAD_DIM, :] = (
            qt[h * FOX_HEAD_DIM:(h + 1) * FOX_HEAD_DIM, :].astype(BF16))
        qt_ref[0, h * LANES + FOX_HEAD_DIM:(h + 1) * LANES, :] = (
            extra_qt[h * LANES + FOX_HEAD_DIM:(h + 1) * LANES, :].astype(BF16))
    vt_ref[0] = lax.dot_general(wvt_ref[...], x, NT_DIMS, preferred_element_type=F32).astype(BF16)


def _proj(h, wqt, wvt, w_big, pqt, pk, bias_row, tables, qn, kvn, wuqt, wuk, wuvt, n_real, tpb,
          n_tiles, ln_in=None):
    first = ln_in is not None
    rows = n_tiles * T
    cos_t, sin_t, cos_tt, sin_tt = tables
    half = MLA_ROPE_DIM // 2
    row = lambda i: (i, 0)
    full = lambda i: (0, 0)
    blk = lambda i: (i, 0, 0)
    tab = lambda i: jnp.where(i < n_real, i % tpb, tpb)
    operands = (
        jax.ShapeDtypeStruct((n_tiles, WIDE, T), BF16),
        jax.ShapeDtypeStruct((n_tiles, T, WIDE), BF16),
        jax.ShapeDtypeStruct((n_tiles, N_HEADS * MLA_V_DIM, T), BF16),
    )
    operand_specs = (
        pl.BlockSpec((1, WIDE, T), blk),
        pl.BlockSpec((1, T, WIDE), blk),
        pl.BlockSpec((1, N_HEADS * MLA_V_DIM, T), blk),
    )
    out_shape = operands + (jax.ShapeDtypeStruct((rows, 2 * D_MODEL), BF16),) + operands
    out_specs = operand_specs + (pl.BlockSpec((T, 2 * D_MODEL), row),) + operand_specs
    if first:
        x_specs = [
            pl.BlockSpec((T, D_MODEL), lambda i: (jnp.minimum(i, n_real - 1), 0)),
            pl.BlockSpec((N_META, D_MODEL), full),
            pl.BlockSpec((1, D_MODEL), full),
            pl.BlockSpec((1, D_MODEL), full),
        ]
        x_args = (h,) + tuple(ln_in)
        out_shape += (jax.ShapeDtypeStruct((rows, D_MODEL), F32),)
        out_specs += (pl.BlockSpec((T, D_MODEL), row),)
    else:
        x_specs = [pl.BlockSpec((T, D_MODEL), row)]
        x_args = (h,)
    return pl.pallas_call(
        functools.partial(_proj_kernel, n_real=n_real, tpb=tpb, first=first),
        out_shape=out_shape,
        grid=(n_tiles,),
        in_specs=x_specs + [
            pl.BlockSpec((FOX_WIDTH, D_MODEL), full),
            pl.BlockSpec((FOX_WIDTH, D_MODEL), full),
            pl.BlockSpec((D_MODEL, PROJ_COLS), full),
            pl.BlockSpec((WIDE, LANES), full),
            pl.BlockSpec((LANES, WIDE), full),
            pl.BlockSpec((1, LANES), full),
            pl.BlockSpec((T, LANES), lambda i: (tab(i), 0)),
            pl.BlockSpec((T, LANES), lambda i: (tab(i), 0)),
            pl.BlockSpec((half, T), lambda i: (0, tab(i))),
            pl.BlockSpec((half, T), lambda i: (0, tab(i))),
            pl.BlockSpec((1, MLA_Q_RANK), full),
            pl.BlockSpec((1, MLA_KV_RANK), full),
            pl.BlockSpec((WIDE, MLA_Q_RANK), full),
            pl.BlockSpec((MLA_KV_RANK, WIDE), full),
            pl.BlockSpec((MLA_WIDTH, MLA_KV_RANK), full),
        ],
        out_specs=out_specs,
        scratch_shapes=[pltpu.VMEM((1, LANES), F32), pltpu.VMEM((T, LANES), F32)],
        compiler_params=_cparams(("arbitrary",)),
        name="in_proj",
    )(*x_args, wqt, wvt, w_big, pqt, pk, bias_row, cos_t, sin_t, cos_tt, sin_tt, qn, kvn, wuqt, wuk, wuvt)


def _decay_placement():
    pk = [[0.0] * WIDE for _ in range(LANES)]
    pqt = [[0.0] * LANES for _ in range(WIDE)]
    for h in range(N_HEADS):
        base = h * LANES
        for s, e in enumerate((E_HI, E_MID, E_LO)):
            pqt[base + X_CQ + s][e + h] = 1.0
            pk[E_ONE][base + X_CQ + s] = 1.0
            pqt[base + X_ONE_K + s][E_ONE] = 1.0
            pk[e + h][base + X_ONE_K + s] = -1.0
        pqt[base + X_BIAS][E_ONE] = 1.0
        pk[E_PAD][base + X_BIAS] = NEG_INF
    return jnp.array(pqt, F32).astype(BF16), jnp.array(pk, F32).astype(BF16)


def _attn_kernel(qt_ref, k_ref, vt_ref, qtm_ref, km_ref, vtm_ref, o_ref, om_ref, st_ref, stm_ref):
    nh = ATTN_HEADS
    n_blocks = qt_ref.shape[0]
    km = km_ref[0]
    vtm = vtm_ref[0]

    def causal(n):
        return (lax.broadcasted_iota(jnp.int32, (n, n), 0)
                <= lax.broadcasted_iota(jnp.int32, (n, n), 1))

    def head(a, jj, width):
        return a[jj * width:(jj + 1) * width]

    qtm = qtm_ref[0]
    outs = []
    for jj in range(nh):
        st = jnp.dot(km[:, jj * LANES:(jj + 1) * LANES], head(qtm, jj, LANES),
                     preferred_element_type=F32)
        st = jnp.where(causal(BLOCK), st, NEG_INF)
        p = jnp.exp2(st - jnp.max(st, axis=0, keepdims=True))
        pv = jnp.dot(head(vtm, jj, MLA_V_DIM), p.astype(BF16), preferred_element_type=F32)
        outs.append(pv / jnp.sum(p, axis=0, keepdims=True))
    om_ref[...] = jnp.concatenate(outs, axis=0).T.astype(om_ref.dtype)

    def query_block(qi, carry):
        qt = qt_ref[qi]

        def scores(kj, slot):
            k = k_ref[kj]
            for jj in range(nh):
                st_ref[slot, jj] = jnp.dot(k[:, jj * LANES:(jj + 1) * LANES], head(qt, jj, LANES),
                                           preferred_element_type=F32)

        def consume(kj, slot, state, diagonal):
            vt = vt_ref[kj]
            out = []
            for jj in range(nh):
                m, l, acc = state[jj]
                st = st_ref[slot, jj]
                m_new = m
                if diagonal:
                    st = jnp.where(causal(T), st, NEG_INF)
                    stm = stm_ref[jj]
                    m_new = jnp.maximum(m_new, jnp.max(stm, axis=0, keepdims=True))
                m_new = jnp.maximum(m_new, jnp.max(st, axis=0, keepdims=True))
                alpha = jnp.exp2(m - m_new)
                p = jnp.exp2(st - m_new)
                l_new = alpha * l + jnp.sum(p, axis=0, keepdims=True)
                pv = jnp.dot(head(vt, jj, MLA_V_DIM), p.astype(BF16), preferred_element_type=F32)
                if diagonal:
                    pm = jnp.exp2(stm - m_new)
                    l_new = l_new + jnp.sum(pm, axis=0, keepdims=True)
                    pv = pv + jnp.dot(head(vtm, jj, MLA_V_DIM), pm.astype(BF16),
                                      preferred_element_type=F32)
                out.append((m_new, l_new, alpha * acc + pv))
            return tuple(out)

        init_one = (jnp.full((1, T), NEG_INF, F32), jnp.zeros((1, T), F32),
                    jnp.zeros((MLA_V_DIM, T), F32))
        state = (init_one,) * nh

        def pair(i, state):
            c0 = 2 * i
            scores(c0 + 1, 1)
            state = consume(c0, 0, state, False)
            scores(c0 + 2, 0)
            return consume(c0 + 1, 1, state, False)

        def odd_tail(state):
            scores(qi, 1)
            state = consume(qi - 1, 0, state, False)
            return consume(qi, 1, state, True)

        def even_tail(state):
            return consume(qi, 0, state, True)

        scores(0, 0)
        for jj in range(nh):
            stm_ref[jj] = jnp.dot(km[:, jj * LANES:(jj + 1) * LANES], head(qt, jj, LANES),
                                  preferred_element_type=F32)
        state = lax.fori_loop(0, qi // 2, pair, state)
        state = lax.cond(qi % 2 == 1, odd_tail, even_tail, state)
        ot = jnp.concatenate([acc / l for (_, l, acc) in state], axis=0)
        o_ref[pl.ds(pl.multiple_of(qi * T, T), T), :] = ot.T.astype(o_ref.dtype)
        return carry

    lax.fori_loop(0, n_blocks, query_block, 0)


def _attention(qt, k, vt, batch, n_real, tpb, name):
    nh = ATTN_HEADS
    qk_w, v_w = nh * LANES, nh * MLA_V_DIM
    meta_tile = lambda b: n_real + b // META_PER_TILE
    meta_blk = lambda b: b % META_PER_TILE
    return pl.pallas_call(
        _attn_kernel,
        out_shape=(
            jax.ShapeDtypeStruct((n_real * T, N_HEADS * MLA_V_DIM), BF16),
            jax.ShapeDtypeStruct((batch * BLOCK, N_HEADS * MLA_V_DIM), BF16),
        ),
        grid=(batch, N_HEADS // nh),
        in_specs=[
            pl.BlockSpec((tpb, qk_w, T), lambda b, hg: (b, hg, 0)),
            pl.BlockSpec((tpb, T, qk_w), lambda b, hg: (b, 0, hg)),
            pl.BlockSpec((tpb, v_w, T), lambda b, hg: (b, hg, 0)),
            pl.BlockSpec((1, qk_w, BLOCK), lambda b, hg: (meta_tile(b), hg, meta_blk(b))),
            pl.BlockSpec((1, BLOCK, qk_w), lambda b, hg: (meta_tile(b), meta_blk(b), hg)),
            pl.BlockSpec((1, v_w, BLOCK), lambda b, hg: (meta_tile(b), hg, meta_blk(b))),
        ],
        out_specs=(
            pl.BlockSpec((tpb * T, v_w), lambda b, hg: (b, hg)),
            pl.BlockSpec((BLOCK, v_w), lambda b, hg: (b, hg)),
        ),
        scratch_shapes=[pltpu.VMEM((2, nh, T, T), F32),
                        pltpu.VMEM((nh, BLOCK, T), F32)],
        compiler_params=_cparams(("parallel", "parallel")),
        name=name,
    )(qt, k, vt, qt, k, vt)


def _mla_operands(cq, ckv, misc, is_pad, cos, sin, cost, sint, qn_g, kvn_g, wuqt, wuk, wuvt,
                  qt_ref, k_ref, vt_ref):
    half = MLA_ROPE_DIM // 2

    qn = _rms_norm(cq, qn_g).astype(BF16)
    qt = lax.dot_general(wuqt, qn, NT_DIMS, preferred_element_type=F32)
    qt = qt * ((MLA_NOPE_DIM + MLA_ROPE_DIM) ** -0.5 * LOG2E)
    tail =jnp.where(lax.broadcasted_iota(jnp.int32, (LANES - M_BIAS, T), 0) == 0, 1.0, 0.0)
    for h in range(N_HEADS):
        base = h * LANES
        x1 = qt[base + MLA_NOPE_DIM:base + MLA_NOPE_DIM + half, :]
        x2 = qt[base + MLA_NOPE_DIM + half:base + M_BIAS, :]
        qt_ref[0, base:base + MLA_NOPE_DIM, :] = qt[base:base + MLA_NOPE_DIM, :].astype(BF16)
        qt_ref[0, base + MLA_NOPE_DIM:base + MLA_NOPE_DIM + half, :] = (x1 * cost - x2 * sint).astype(BF16)
        qt_ref[0, base + MLA_NOPE_DIM + half:base + M_BIAS, :] = (x2 * cost + x1 * sint).astype(BF16)
        qt_ref[0, base + M_BIAS:base + LANES, :] = tail.astype(BF16)

    kvn = _rms_norm(ckv, kvn_g).astype(BF16)
    k_nope = jnp.dot(kvn, wuk, preferred_element_type=F32)
    lane = lax.broadcasted_iota(jnp.int32, (T, LANES), 1)
    kr = jnp.where((lane >= KR_LANE) & (lane < KR_LANE + MLA_ROPE_DIM), misc, 0.0)
    from_hi = pltpu.roll(kr, LANES - half, 1)
    from_lo = pltpu.roll(kr, half, 1)
    swapped = jnp.where(lane < KR_LANE + half, -from_hi, from_lo)
    k_rot = kr * cos + swapped * sin
    k_rot = jnp.where((lane == M_BIAS) & is_pad, NEG_INF, k_rot)
    k_ref[0] = (k_nope + jnp.concatenate([k_rot] * N_HEADS, axis=1)).astype(BF16)
    vt_ref[0] = lax.dot_general(wuvt, kvn, NT_DIMS, preferred_element_type=F32).astype(BF16)


def _top2_sum(a, b, c, d):
    hi1, lo1 = jnp.maximum(a, b), jnp.minimum(a, b)
    hi2, lo2 = jnp.maximum(c, d), jnp.minimum(c, d)
    return jnp.maximum(hi1, hi2) + jnp.maximum(jnp.minimum(hi1, hi2), jnp.maximum(lo1, lo2))


def _route(logits_t, bias_col):
    scores = jax.nn.sigmoid(logits_t)
    biased = scores + bias_col
    b = [biased[e:e + 1, :] for e in range(N_EXPERTS)]
    s = [scores[e:e + 1, :] for e in range(N_EXPERTS)]
    gscore = [_top2_sum(*b[EXPERTS_PER_GROUP * g:EXPERTS_PER_GROUP * (g + 1)]) for g in range(N_GROUPS)]
    best = gscore[0]
    gidx = jnp.zeros_like(best, dtype=jnp.int32)
    for g in range(1, N_GROUPS):
        better = gscore[g] > best
        gidx = jnp.where(better, g, gidx)
        best = jnp.where(better, gscore[g], best)
    in_g = [gidx == g for g in range(N_GROUPS)]

    def pick(vals, j):
        out = vals[j]
        for g in range(1, N_GROUPS):
            out = jnp.where(in_g[g], vals[EXPERTS_PER_GROUP * g + j], out)
        return out

    vb = [pick(b, j) for j in range(EXPERTS_PER_GROUP)]
    vs = [pick(s, j) for j in range(EXPERTS_PER_GROUP)]
    chosen = []
    for j in range(EXPERTS_PER_GROUP):
        rank = jnp.zeros_like(gidx)
        for i in range(EXPERTS_PER_GROUP):
            if i == j:
                continue
            ahead = (vb[i] >= vb[j]) if i < j else (vb[i] > vb[j])
            rank = rank + jnp.where(ahead, 1, 0)
        chosen.append(rank < 2)
    total = sum(jnp.where(chosen[j], vs[j], 0.0) for j in range(EXPERTS_PER_GROUP))
    gates = [jnp.where(chosen[j], vs[j] / total, 0.0) for j in range(EXPERTS_PER_GROUP)]
    rows = []
    for g in range(N_GROUPS):
        for j in range(EXPERTS_PER_GROUP):
            rows.append(jnp.where(in_g[g], gates[j], 0.0))
    return jnp.concatenate(rows, axis=0), gidx


def _post_kernel(of_ref, ofm_ref, om_ref, omm_ref, g_ref, h_ref, wfo_ref, wmo_ref, wout_ref,
                 lng_ref, lnb_ref, rwh_ref, rwl_ref, rb_ref, h1_ref, comb_ref, gid_ref,
                 *, alpha, n_real_tiles):
    is_meta = pl.program_id(0) >= n_real_tiles
    o_fox = jnp.where(is_meta, ofm_ref[...], of_ref[...])
    o_mla = jnp.where(is_meta, omm_ref[...], om_ref[...])
    y_fox = jnp.dot(o_fox, wfo_ref[...], preferred_element_type=F32)
    y_mla = jnp.dot(o_mla, wmo_ref[...], preferred_element_type=F32)
    merged = (jax.nn.sigmoid(g_ref[:, :D_MODEL].astype(F32)) * y_fox
              + jax.nn.sigmoid(g_ref[:, D_MODEL:].astype(F32)) * y_mla)
    mix = jnp.dot(merged.astype(BF16), wout_ref[...], preferred_element_type=F32)
    h1 = _layer_norm(alpha * h_ref[...] + mix, lng_ref[...], lnb_ref[...])
    h1_ref[...] = h1
    h_hi = h1.astype(BF16)
    h_lo = (h1 - h_hi.astype(F32)).astype(BF16)
    rwh, rwl = rwh_ref[...], rwl_ref[...]
    both = lax.dot_general(jnp.concatenate([rwh, rwl], axis=0), h_hi, NT_DIMS,
                           preferred_element_type=F32)
    logits_t = (both[:N_EXPERTS] + both[N_EXPERTS:]
                + lax.dot_general(rwh, h_lo, NT_DIMS, preferred_element_type=F32))
    comb_ref[...], gid_ref[...] = _route(logits_t, rb_ref[...])


def _post(o_fox, o_mla, g, h, wfo, wmo, wout, lng, lnb, rwh, rwl, rb, alpha):
    rows = h.shape[0]
    tm = TM_POST
    o_fox_r, o_fox_m = o_fox
    o_mla_r, o_mla_m = o_mla
    nrt = o_fox_r.shape[0] // tm
    row = lambda i: (i, 0)
    real = lambda i: (jnp.minimum(i, nrt - 1), 0)
    meta = lambda i: (jnp.maximum(i - nrt, 0), 0)
    full = lambda i: (0, 0)
    return pl.pallas_call(
        functools.partial(_post_kernel, alpha=alpha, n_real_tiles=nrt),
        out_shape=(
            jax.ShapeDtypeStruct((rows, D_MODEL), F32),
            jax.ShapeDtypeStruct((N_EXPERTS, rows), F32),
            jax.ShapeDtypeStruct((1, rows), jnp.int32),
        ),
        grid=(rows // tm,),
        in_specs=[
            pl.BlockSpec((tm, FOX_WIDTH), real),
            pl.BlockSpec((tm, FOX_WIDTH), meta),
            pl.BlockSpec((tm, MLA_WIDTH), real),
            pl.BlockSpec((tm, MLA_WIDTH), meta),
            pl.BlockSpec((tm, 2 * D_MODEL), row),
            pl.BlockSpec((tm, D_MODEL), row),
            pl.BlockSpec((FOX_WIDTH, D_MODEL), full),
            pl.BlockSpec((MLA_WIDTH, D_MODEL), full),
            pl.BlockSpec((D_MODEL, D_MODEL), full),
            pl.BlockSpec((1, D_MODEL), full),
            pl.BlockSpec((1, D_MODEL), full),
            pl.BlockSpec((N_EXPERTS, D_MODEL), full),
            pl.BlockSpec((N_EXPERTS, D_MODEL), full),
            pl.BlockSpec((N_EXPERTS, 1), full),
        ],
        out_specs=(
            pl.BlockSpec((tm, D_MODEL), row),
            pl.BlockSpec((N_EXPERTS, tm), lambda i: (0, i)),
            pl.BlockSpec((1, tm), lambda i: (0, i)),
        ),
        compiler_params=_cparams(("parallel",)),
        name="merge_ln1_router",
    )(o_fox_r, o_fox_m, o_mla_r, o_mla_m, g, h, wfo, wmo, wout, lng, lnb, rwh, rwl, rb)


def _moe_kernel(cnt_ref, h_ref, comb_ref, gid_ref, wg_ref, wu_ref, wd_ref, lng_ref, lnb_ref, o_ref,
                p_ref, pt_ref, xs_ref, cws_ref, ys_ref, *, alpha):
    i = pl.program_id(0)
    g = pl.program_id(1)
    tm, slots = h_ref.shape[0], p_ref.shape[0]

    def chunks(gg):
        return lax.shift_right_logical(cnt_ref[i * N_GROUPS + gg] + (MOE_CHUNK - 1), MOE_CHUNK_LOG2)

    @pl.when(g == 0)
    def _():
        gid = gid_ref[...]
        sub = lax.broadcasted_iota(jnp.int32, (8, tm), 0)
        lane = lax.broadcasted_iota(jnp.int32, (8, tm), 1)
        onehot = jnp.where(gid == sub, 1, 0)
        cum = onehot
        shift = 1
        while shift < tm:
            cum = cum + jnp.where(lane >= shift, pltpu.roll(cum, shift, 1), 0)
            shift *= 2
        slot = jnp.zeros((1, tm), jnp.int32)
        off = jnp.int32(0)
        for gg in range(N_GROUPS):
            slot = slot + onehot[gg:gg + 1, :] * (cum[gg:gg + 1, :] - 1 + off)
            off = off + chunks(gg) * MOE_CHUNK
        srow = lax.broadcasted_iota(jnp.int32, (slots, tm), 0)
        p = jnp.where(srow == slot, 1.0, 0.0).astype(BF16)
        p_ref[...] = p
        stack = jnp.concatenate([comb_ref[...], slot.astype(F32),
                                 jnp.zeros((LANES - N_EXPERTS - 1, tm), F32)], axis=0)
        nat = stack.T
        lane_n = lax.broadcasted_iota(jnp.int32, (tm, LANES), 1)
        cw = jnp.where(lane_n < N_EXPERTS, nat, 0.0)
        slot_col = jnp.sum(jnp.where(lane_n == N_EXPERTS, nat, 0.0), axis=1, keepdims=True)
        scol = lax.broadcasted_iota(jnp.int32, (tm, slots), 1)
        pt_ref[...] = jnp.where(scol == slot_col.astype(jnp.int32), 1.0, 0.0).astype(BF16)
        xs_ref[...] = jnp.dot(p, h_ref[...].astype(BF16), preferred_element_type=F32).astype(BF16)
        cw_hi = cw.astype(BF16)
        cw_lo = (cw - cw_hi.astype(F32)).astype(BF16)
        both = jnp.dot(p, jnp.concatenate([cw_hi, cw_lo], axis=1), preferred_element_type=F32)
        cws_ref[...] = both[:, :LANES] + both[:, LANES:]
        ys_ref[...] = jnp.zeros_like(ys_ref)

    off_g = jnp.int32(0)
    for gg in range(N_GROUPS - 1):
        off_g = off_g + jnp.where(gg < g, chunks(gg), 0) * MOE_CHUNK
    lane_c = lax.broadcasted_iota(jnp.int32, (MOE_CHUNK, LANES), 1)

    def chunk(k, carry):
        r0 = pl.multiple_of(off_g + k * MOE_CHUNK, MOE_CHUNK)
        x = xs_ref[pl.ds(r0, MOE_CHUNK), :]
        gate = jnp.dot(x, wg_ref[...], preferred_element_type=F32)
        up = jnp.dot(x, wu_ref[...], preferred_element_type=F32)
        hid = gate * jax.nn.sigmoid(gate) * up
        cw = cws_ref[pl.ds(r0, MOE_CHUNK), :]
        pieces = []
        for e in range(EXPERTS_PER_GROUP):
            w_e = jnp.sum(jnp.where(lane_c == g * EXPERTS_PER_GROUP + e, cw, 0.0), axis=1, keepdims=True)
            pieces.append((hid[:, e * D_EXPERT:(e + 1) * D_EXPERT] * w_e).astype(BF16))
        y = jnp.dot(jnp.concatenate(pieces, axis=1), wd_ref[...], preferred_element_type=F32)
        ys_ref[pl.ds(r0, MOE_CHUNK), :] = y.astype(BF16)
        return carry

    lax.fori_loop(0, chunks(g), chunk, 0)

    @pl.when(g == N_GROUPS - 1)
    def _():
        y = jnp.dot(pt_ref[...], ys_ref[...], preferred_element_type=F32)
        o_ref[...] = _layer_norm(alpha * h_ref[...] + y, lng_ref[...], lnb_ref[...])


def _moe(h1, comb_t, gid, wg, wu, wd, lng, lnb, alpha, n_tiles):
    tm = TM_MOE
    slots = tm + N_GROUPS * MOE_CHUNK
    hidden = EXPERTS_PER_GROUP * D_EXPERT
    tile_gid = gid[0, :n_tiles * tm].reshape(n_tiles, tm)
    cnt = jnp.sum(tile_gid[:, :, None] == jnp.arange(N_GROUPS, dtype=jnp.int32), axis=1,
                  dtype=jnp.int32).reshape(-1)
    return pl.pallas_call(
        functools.partial(_moe_kernel, alpha=alpha),
        out_shape=jax.ShapeDtypeStruct((n_tiles * tm, D_MODEL), F32),
        grid_spec=pltpu.PrefetchScalarGridSpec(
            num_scalar_prefetch=1,
            grid=(n_tiles, N_GROUPS),
            in_specs=[
                pl.BlockSpec((tm, D_MODEL), lambda i, g, cnt: (i, 0)),
                pl.BlockSpec((N_EXPERTS, tm), lambda i, g, cnt: (0, i)),
                pl.BlockSpec((1, tm), lambda i, g, cnt: (0, i)),
                pl.BlockSpec((D_MODEL, hidden), lambda i, g, cnt: (0, g)),
                pl.BlockSpec((D_MODEL, hidden), lambda i, g, cnt: (0, g)),
                pl.BlockSpec((hidden, D_MODEL), lambda i, g, cnt: (g, 0)),
                pl.BlockSpec((1, D_MODEL), lambda i, g, cnt: (0, 0)),
                pl.BlockSpec((1, D_MODEL), lambda i, g, cnt: (0, 0)),
            ],
            out_specs=pl.BlockSpec((tm, D_MODEL), lambda i, g, cnt: (i, 0)),
            scratch_shapes=[
                pltpu.VMEM((slots, tm), BF16),
                pltpu.VMEM((tm, slots), BF16),
                pltpu.VMEM((slots, D_MODEL), BF16),
                pltpu.VMEM((slots, LANES), F32),
                pltpu.VMEM((slots, D_MODEL), BF16),
            ],
        ),
        compiler_params=_cparams(("parallel", "arbitrary")),
        name="moe_ln2",
    )(cnt, h1, comb_t, gid, wg, wu, wd, lng, lnb)


def _rope_tables(seq):
    half = MLA_ROPE_DIM // 2
    blk = np.arange(T) % BLOCK
    pos = np.concatenate([np.arange(seq) + N_META, np.where(blk < N_META, blk, 0)]).astype(np.float64)
    rows = seq + T
    inv = ROPE_THETA ** (-np.arange(half, dtype=np.float64) / half)
    ang = pos[:, None] * inv[None, :]
    cos, sin = np.cos(ang), np.sin(ang)
    ones = np.ones((rows, KR_LANE))
    tail = LANES - KR_LANE - MLA_ROPE_DIM
    cos_t = np.concatenate([ones, cos, cos, np.ones((rows, tail))], axis=1)
    sin_t = np.concatenate([0 * ones, sin, sin, np.zeros((rows, tail))], axis=1)
    return tuple(jnp.asarray(a, F32) for a in (cos_t, sin_t, cos.T, sin.T))


def _pad_heads(w, n_heads, per_head, keep_lo, keep_hi):
    k = w.shape[0]
    w = w.reshape(k, n_heads, per_head)[:, :, keep_lo:keep_hi]
    w = jnp.pad(w, ((0, 0), (0, 0), (0, LANES - (keep_hi - keep_lo))))
    return w.reshape(k, n_heads * LANES)


def kernel(x, meta_tokens, ln_in_g, ln_in_b, w_in, fox_f_bias, fox_w_o, mla_q_norm, mla_w_uq,
           mla_kv_norm, mla_w_ukv, mla_w_o, w_out, ln1_g, ln1_b, router_w, router_b,
           w_gate, w_up, w_down, ln2_g, ln2_b):
    batch, seq, _ = x.shape
    depth = w_in.shape[0]
    assert seq % T == 0 and batch % META_PER_TILE == 0
    tpb = seq // T
    n_real = batch * tpb
    n_tiles = n_real + batch // META_PER_TILE
    assert (n_real * T) % TM_MOE == 0 and (n_tiles * T) % TM_MOE == 0
    alpha = (2 * depth) ** 0.25
    row = lambda a: a.reshape(1, -1).astype(F32)

    h = x.reshape(batch * seq, D_MODEL)
    ln_in = (meta_tokens.astype(F32), row(ln_in_g), row(ln_in_b))
    tables = _rope_tables(seq)
    pqt, pk = _decay_placement()
    rw_t = router_w.T.astype(F32)
    rw_hi = rw_t.astype(BF16)
    rw_lo = (rw_t - rw_hi.astype(F32)).astype(BF16)
    rb = router_b.reshape(N_EXPERTS, 1).astype(F32)

    o_q = FOX_WIDTH
    o_k = o_q + FOX_WIDTH
    o_v = o_k + FOX_WIDTH
    o_f = o_v + FOX_HEADS
    o_cq = o_f + MLA_Q_RANK
    o_ckv = o_cq + MLA_KV_RANK
    o_kr = o_ckv + MLA_ROPE_DIM
    for i in range(depth):
        w = w_in[i]
        zeros = lambda n: jnp.zeros((D_MODEL, n), w.dtype)
        w_misc = jnp.concatenate([w[:, o_v:o_f], zeros(KR_LANE - FOX_HEADS), w[:, o_ckv:o_kr],
                                  zeros(LANES - KR_LANE - MLA_ROPE_DIM)], axis=1)
        w_k = _pad_heads(w[:, o_q:o_k], FOX_HEADS, FOX_HEAD_DIM, 0, FOX_HEAD_DIM)
        w_big = jnp.concatenate([w_k, w[:, o_f:o_ckv], w[:, o_kr:], w_misc], axis=1).astype(BF16)
        wqt = w[:, :o_q].T.astype(BF16)
        wvt = w[:, o_k:o_v].T.astype(BF16)
        bias_row = jnp.pad(fox_f_bias[i].astype(F32), (0, LANES - FOX_HEADS)).reshape(1, LANES)
        qk_dim = MLA_NOPE_DIM + MLA_ROPE_DIM
        wuqt = _pad_heads(mla_w_uq[i], MLA_HEADS, qk_dim, 0, qk_dim).T.astype(BF16)
        kv_dim = MLA_NOPE_DIM + MLA_V_DIM
        wuk = _pad_heads(mla_w_ukv[i], MLA_HEADS, kv_dim, 0, MLA_NOPE_DIM).astype(BF16)
        wuv = mla_w_ukv[i].reshape(MLA_KV_RANK, MLA_HEADS, kv_dim)[:, :, MLA_NOPE_DIM:]
        wuvt = wuv.reshape(MLA_KV_RANK, MLA_WIDTH).T.astype(BF16)
        outs = _proj(h, wqt, wvt, w_big, pqt, pk, bias_row, tables, row(mla_q_norm[i]),
                     row(mla_kv_norm[i]), wuqt, wuk, wuvt, n_real, tpb, n_tiles,
                     ln_in=ln_in if i == 0 else None)
        qt_f, k_f, vt_f, g, qt_m, k_m, vt_m = outs[:7]
        if i == 0:
            h = outs[7]
        o_fox = _attention(qt_f, k_f, vt_f, batch, n_real, tpb, "fox_attn")
        o_mla = _attention(qt_m, k_m, vt_m, batch, n_real, tpb, "mla_attn")

        h1, comb_t, gid = _post(o_fox, o_mla, g, h, fox_w_o[i].astype(BF16), mla_w_o[i].astype(BF16),
                                w_out[i].astype(BF16), row(ln1_g[i]), row(ln1_b[i]), rw_hi, rw_lo, rb,
                                alpha)

        wg = jnp.transpose(w_gate[i], (1, 0, 2)).reshape(D_MODEL, N_EXPERTS * D_EXPERT).astype(BF16)
        wu = jnp.transpose(w_up[i], (1, 0, 2)).reshape(D_MODEL, N_EXPERTS * D_EXPERT).astype(BF16)
        wd = w_down[i].reshape(N_EXPERTS * D_EXPERT, D_MODEL).astype(BF16)
        moe_rows = (n_real if i == depth - 1 else n_tiles) * T
        h = _moe(h1, comb_t, gid, wg, wu, wd, row(ln2_g[i]), row(ln2_b[i]), alpha, moe_rows // TM_MOE)

    return h.reshape(batch, seq, D_MODEL)
```

```python
import functools
import math

import jax
import jax.numpy as jnp
import numpy as np
from jax import lax
from jax.experimental import pallas as pl
from jax.experimental.pallas import tpu as pltpu

F32 = jnp.float32
BF16 = jnp.bfloat16

D_MODEL = 1024
N_META = 16
BLOCK = 128
NEG_INF = -1e30
LOG2E = math.log2(math.e)

FOX_HEADS = 8
FOX_HEAD_DIM = 64
FOX_WIDTH = FOX_HEADS * FOX_HEAD_DIM

MLA_HEADS = 8
MLA_NOPE_DIM = 64
MLA_ROPE_DIM = 32
MLA_V_DIM = 64
MLA_Q_RANK = 384
MLA_KV_RANK = 256
MLA_WIDTH = MLA_HEADS * MLA_V_DIM
ROPE_THETA = 10000.0

N_EXPERTS = 16
N_GROUPS = 4
EXPERTS_PER_GROUP = N_EXPERTS // N_GROUPS
D_EXPERT = 256

LN_EPS = 1e-5
RMS_EPS = 1e-6

LANES = 128
N_HEADS = FOX_HEADS
ATTN_HEADS = 4
WIDE = N_HEADS * LANES
KR_LANE = 64

E_HI, E_MID, E_LO, E_ONE, E_PAD = 0, N_HEADS, 2 * N_HEADS, 3 * N_HEADS, 3 * N_HEADS + 1
X_CQ, X_ONE_K, X_BIAS = FOX_HEAD_DIM, FOX_HEAD_DIM + 3, FOX_HEAD_DIM + 6
X_ROWS = 8
M_BIAS = MLA_NOPE_DIM + MLA_ROPE_DIM

C_K = 0
C_CQ = WIDE
C_CKV = C_CQ + MLA_Q_RANK
C_G = C_CKV + MLA_KV_RANK
C_MISC = C_G + 2 * D_MODEL
PROJ_COLS = C_MISC + LANES

VMEM_LIMIT = 56 * 1024 * 1024

T = 512
META_PER_TILE = T // BLOCK
TM_POST = 512
TM_MOE = 1024
MOE_CHUNK_LOG2 = 7
MOE_CHUNK = 1 << MOE_CHUNK_LOG2

NT_DIMS = (((1,), (1,)), ((), ()))


def _cparams(sem):
    return pltpu.CompilerParams(dimension_semantics=sem, vmem_limit_bytes=VMEM_LIMIT)


def _layer_norm(x, g, b):
    mu = jnp.mean(x, axis=-1, keepdims=True)
    xc = x - mu
    var = jnp.mean(xc * xc, axis=-1, keepdims=True)
    return xc * lax.rsqrt(var + LN_EPS) * g + b


def _rms_norm(x, g):
    ms = jnp.mean(x * x, axis=-1, keepdims=True)
    return x * lax.rsqrt(ms + RMS_EPS) * g


def _proj_kernel(*refs, n_real, tpb, first):
    if first:
        x_ref, meta_ref, lng_ref, lnb_ref = refs[:4]
        refs = refs[4:]
    else:
        x_ref, refs = refs[0], refs[1:]
    (wqt_ref, wvt_ref, w_ref, pqt_ref, pk_ref, bias_ref, cos_ref, sin_ref, cost_ref, sint_ref,
     qn_ref, kvn_ref, wuqt_ref, wuk_ref, wuvt_ref,
     qt_ref, k_ref, vt_ref, g_ref, qtm_ref, km_ref, vtm_ref) = refs[:22]
    carry_ref, c_ref = refs[-2:]
    i = pl.program_id(0)
    if first:
        h_ref = refs[22]

        @pl.when(i < n_real)
        def _():
            h_ref[...] = _layer_norm(x_ref[...], lng_ref[...], lnb_ref[...])

        @pl.when(i >= n_real)
        def _():
            h_ref[...] = jnp.zeros_like(h_ref)
            m = _layer_norm(meta_ref[...], lng_ref[...], lnb_ref[...])
            for jb in range(META_PER_TILE):
                h_ref[jb * BLOCK:jb * BLOCK + N_META, :] = m

        x = h_ref[...].astype(BF16)
    else:
        x = x_ref[...].astype(BF16)

    def mm(lo, hi):
        return jnp.dot(x, w_ref[:, lo:hi], preferred_element_type=F32)

    misc = mm(C_MISC, PROJ_COLS)
    g_ref[:, :D_MODEL] = mm(C_G, C_G + D_MODEL).astype(BF16)
    g_ref[:, D_MODEL:] = mm(C_G + D_MODEL, C_MISC).astype(BF16)

    z = misc + bias_ref[...]
    logf = jnp.minimum(z, 0.0) - jnp.log1p(jnp.exp(-jnp.abs(z)))
    row = lax.broadcasted_iota(jnp.int32, (T, LANES), 0)
    lane = lax.broadcasted_iota(jnp.int32, (T, LANES), 1)
    blk_row = row % BLOCK
    is_meta = i >= n_real

    @pl.when(jnp.logical_not(is_meta))
    def _():
        @pl.when(i % tpb == 0)
        def _():
            carry_ref[...] = jnp.zeros_like(carry_ref)

        c = logf
        shift = 1
        while shift < T:
            c = c + jnp.where(row >= shift, pltpu.roll(c, shift, 0), 0.0)
            shift *= 2
        c = c + carry_ref[...]
        carry_ref[...] = c[T - 1:T, :]
        c_ref[...] = c

    @pl.when(is_meta)
    def _():
        own = jnp.where(blk_row < N_META, logf, 0.0)
        s = own
        shift = 1
        while shift < BLOCK:
            s = s + jnp.where(blk_row < BLOCK - shift, pltpu.roll(s, T - shift, 0), 0.0)
            shift *= 2
        c_ref[...] = own - s

    is_pad = is_meta & (blk_row >= N_META)
    _mla_operands(mm(C_CQ, C_CKV), mm(C_CKV, C_G), misc, is_pad, cos_ref[...], sin_ref[...],
                  cost_ref[...], sint_ref[...], qn_ref[...], kvn_ref[...], wuqt_ref[...],
                  wuk_ref[...], wuvt_ref[...], qtm_ref, km_ref, vtm_ref)
    c2 = c_ref[...] * LOG2E
    hi = c2.astype(BF16).astype(F32)
    r1 = c2 - hi
    mid = r1.astype(BF16).astype(F32)
    lo = (r1 - mid).astype(BF16).astype(F32)
    feat = jnp.where(lane < E_MID, hi,
           jnp.where(lane < E_LO, pltpu.roll(mid, E_MID, 1),
           jnp.where(lane < E_ONE, pltpu.roll(lo, E_LO, 1),
           jnp.where(lane == E_ONE, 1.0,
           jnp.where((lane == E_PAD) & is_pad, 1.0, 0.0)))))
    feat = feat.astype(BF16)
    extra_k = jnp.dot(feat, pk_ref[...], preferred_element_type=F32)
    extra_qt = lax.dot_general(pqt_ref[...], feat, NT_DIMS, preferred_element_type=F32)
    no_extra = jnp.zeros((LANES - FOX_HEAD_DIM - X_ROWS, T), F32)

    k_ref[0] = (mm(C_K, C_CQ) + extra_k).astype(BF16)
    qt = lax.dot_general(wqt_ref[...], x, NT_DIMS, preferred_element_type=F32)
    qt = qt * (FOX_HEAD_DIM ** -0.5 * LOG2E)
    for h in range(N_HEADS):
        qt_ref[0, h * LANES:h * LANES + FOX_HEAD_DIM, :] = (
            qt[h * FOX_HEAD_DIM:(h + 1) * FOX_HEAD_DIM, :].astype(BF16))
        qt_ref[0, h * LANES + FOX_HEAD_DIM:(h + 1) * LANES, :] = jnp.concatenate(
            [extra_qt[h * X_ROWS:(h + 1) * X_ROWS, :], no_extra], axis=0).astype(BF16)
    vt_ref[0] = lax.dot_general(wvt_ref[...], x, NT_DIMS, preferred_element_type=F32).astype(BF16)


def _proj(h, wqt, wvt, w_big, pqt, pk, bias_row, tables, qn, kvn, wuqt, wuk, wuvt, n_real, tpb,
          n_tiles, ln_in=None):
    first = ln_in is not None
    rows = n_tiles * T
    cos_t, sin_t, cos_tt, sin_tt = tables
    half = MLA_ROPE_DIM // 2
    row = lambda i: (i, 0)
    full = lambda i: (0, 0)
    blk = lambda i: (i, 0, 0)
    tab = lambda i: jnp.where(i < n_real, i % tpb, tpb)
    operands = (
        jax.ShapeDtypeStruct((n_tiles, WIDE, T), BF16),
        jax.ShapeDtypeStruct((n_tiles, T, WIDE), BF16),
        jax.ShapeDtypeStruct((n_tiles, N_HEADS * MLA_V_DIM, T), BF16),
    )
    operand_specs = (
        pl.BlockSpec((1, WIDE, T), blk),
        pl.BlockSpec((1, T, WIDE), blk),
        pl.BlockSpec((1, N_HEADS * MLA_V_DIM, T), blk),
    )
    out_shape = operands + (jax.ShapeDtypeStruct((rows, 2 * D_MODEL), BF16),) + operands
    out_specs = operand_specs + (pl.BlockSpec((T, 2 * D_MODEL), row),) + operand_specs
    if first:
        x_specs = [
            pl.BlockSpec((T, D_MODEL), lambda i: (jnp.minimum(i, n_real - 1), 0)),
            pl.BlockSpec((N_META, D_MODEL), full),
            pl.BlockSpec((1, D_MODEL), full),
            pl.BlockSpec((1, D_MODEL), full),
        ]
        x_args = (h,) + tuple(ln_in)
        out_shape += (jax.ShapeDtypeStruct((rows, D_MODEL), F32),)
        out_specs += (pl.BlockSpec((T, D_MODEL), row),)
    else:
        x_specs = [pl.BlockSpec((T, D_MODEL), row)]
        x_args = (h,)
    return pl.pallas_call(
        functools.partial(_proj_kernel, n_real=n_real, tpb=tpb, first=first),
        out_shape=out_shape,
        grid=(n_tiles,),
        in_specs=x_specs + [
            pl.BlockSpec((FOX_WIDTH, D_MODEL), full),
            pl.BlockSpec((FOX_WIDTH, D_MODEL), full),
            pl.BlockSpec((D_MODEL, PROJ_COLS), full),
            pl.BlockSpec((N_HEADS * X_ROWS, LANES), full),
            pl.BlockSpec((LANES, WIDE), full),
            pl.BlockSpec((1, LANES), full),
            pl.BlockSpec((T, LANES), lambda i: (tab(i), 0)),
            pl.BlockSpec((T, LANES), lambda i: (tab(i), 0)),
            pl.BlockSpec((half, T), lambda i: (0, tab(i))),
            pl.BlockSpec((half, T), lambda i: (0, tab(i))),
            pl.BlockSpec((1, MLA_Q_RANK), full),
            pl.BlockSpec((1, MLA_KV_RANK), full),
            pl.BlockSpec((WIDE, MLA_Q_RANK), full),
            pl.BlockSpec((MLA_KV_RANK, WIDE), full),
            pl.BlockSpec((MLA_WIDTH, MLA_KV_RANK), full),
        ],
        out_specs=out_specs,
        scratch_shapes=[pltpu.VMEM((1, LANES), F32), pltpu.VMEM((T, LANES), F32)],
        compiler_params=_cparams(("arbitrary",)),
        name="in_proj",
    )(*x_args, wqt, wvt, w_big, pqt, pk, bias_row, cos_t, sin_t, cos_tt, sin_tt, qn, kvn, wuqt, wuk, wuvt)


def _decay_placement():
    pk = [[0.0] * WIDE for _ in range(LANES)]
    pqt = [[0.0] * LANES for _ in range(N_HEADS * X_ROWS)]
    for h in range(N_HEADS):
        base = h * LANES
        qbase = h * X_ROWS - X_CQ
        for s, e in enumerate((E_HI, E_MID, E_LO)):
            pqt[qbase + X_CQ + s][e + h] = 1.0
            pk[E_ONE][base + X_CQ + s] = 1.0
            pqt[qbase + X_ONE_K + s][E_ONE] = 1.0
            pk[e + h][base + X_ONE_K + s] = -1.0
        pqt[qbase + X_BIAS][E_ONE] = 1.0
        pk[E_PAD][base + X_BIAS] = NEG_INF
    return jnp.array(pqt, F32).astype(BF16), jnp.array(pk, F32).astype(BF16)


def _attn_kernel(qt_ref, k_ref, vt_ref, qtm_ref, km_ref, vtm_ref, o_ref, om_ref, st_ref, stm_ref):
    nh = ATTN_HEADS
    n_blocks = qt_ref.shape[0]
    km = km_ref[0]
    vtm = vtm_ref[0]

    def causal(n):
        return (lax.broadcasted_iota(jnp.int32, (n, n), 0)
                <= lax.broadcasted_iota(jnp.int32, (n, n), 1))

    def head(a, jj, width):
        return a[jj * width:(jj + 1) * width]

    qtm = qtm_ref[0]
    outs = []
    for jj in range(nh):
        st = jnp.dot(km[:, jj * LANES:(jj + 1) * LANES], head(qtm, jj, LANES),
                     preferred_element_type=F32)
        st = jnp.where(causal(BLOCK), st, NEG_INF)
        p = jnp.exp2(st - jnp.max(st, axis=0, keepdims=True))
        pv = jnp.dot(head(vtm, jj, MLA_V_DIM), p.astype(BF16), preferred_element_type=F32)
        outs.append(pv / jnp.sum(p, axis=0, keepdims=True))
    om_ref[...] = jnp.concatenate(outs, axis=0).T.astype(om_ref.dtype)

    def query_block(qi, carry):
        qt = qt_ref[qi]

        def scores(kj, slot):
            k = k_ref[kj]
            for jj in range(nh):
                st_ref[slot, jj] = jnp.dot(k[:, jj * LANES:(jj + 1) * LANES], head(qt, jj, LANES),
                                           preferred_element_type=F32)

        def consume(kj, slot, state, diagonal):
            vt = vt_ref[kj]
            out = []
            for jj in range(nh):
                m, l, acc = state[jj]
                st = st_ref[slot, jj]
                m_new = m
                if diagonal:
                    st = jnp.where(causal(T), st, NEG_INF)
                    stm = stm_ref[jj]
                    m_new = jnp.maximum(m_new, jnp.max(stm, axis=0, keepdims=True))
                m_new = jnp.maximum(m_new, jnp.max(st, axis=0, keepdims=True))
                alpha = jnp.exp2(m - m_new)
                p = jnp.exp2(st - m_new)
                l_new = alpha * l + jnp.sum(p, axis=0, keepdims=True)
                pv = jnp.dot(head(vt, jj, MLA_V_DIM), p.astype(BF16), preferred_element_type=F32)
                if diagonal:
                    pm = jnp.exp2(stm - m_new)
                    l_new = l_new + jnp.sum(pm, axis=0, keepdims=True)
                    pv = pv + jnp.dot(head(vtm, jj, MLA_V_DIM), pm.astype(BF16),
                                      preferred_element_type=F32)
                out.append((m_new, l_new, alpha * acc + pv))
            return tuple(out)

        init_one = (jnp.full((1, T), NEG_INF, F32), jnp.zeros((1, T), F32),
                    jnp.zeros((MLA_V_DIM, T), F32))
        state = (init_one,) * nh

        def pair(i, state):
            c0 = 2 * i
            scores(c0 + 1, 1)
            state = consume(c0, 0, state, False)
            scores(c0 + 2, 0)
            return consume(c0 + 1, 1, state, False)

        def odd_tail(state):
            scores(qi, 1)
            state = consume(qi - 1, 0, state, False)
            return consume(qi, 1, state, True)

        def even_tail(state):
            return consume(qi, 0, state, True)

        scores(0, 0)
        for jj in range(nh):
            stm_ref[jj] = jnp.dot(km[:, jj * LANES:(jj + 1) * LANES], head(qt, jj, LANES),
                                  preferred_element_type=F32)
        state = lax.fori_loop(0, qi // 2, pair, state)
        state = lax.cond(qi % 2 == 1, odd_tail, even_tail, state)
        ot = jnp.concatenate([acc / l for (_, l, acc) in state], axis=0)
        o_ref[pl.ds(pl.multiple_of(qi * T, T), T), :] = ot.T.astype(o_ref.dtype)
        return carry

    lax.fori_loop(0, n_blocks, query_block, 0)


def _attention(qt, k, vt, batch, n_real, tpb, name):
    nh = ATTN_HEADS
    qk_w, v_w = nh * LANES, nh * MLA_V_DIM
    meta_tile = lambda b: n_real + b // META_PER_TILE
    meta_blk = lambda b: b % META_PER_TILE
    return pl.pallas_call(
        _attn_kernel,
        out_shape=(
            jax.ShapeDtypeStruct((n_real * T, N_HEADS * MLA_V_DIM), BF16),
            jax.ShapeDtypeStruct((batch * BLOCK, N_HEADS * MLA_V_DIM), BF16),
        ),
        grid=(batch, N_HEADS // nh),
        in_specs=[
            pl.BlockSpec((tpb, qk_w, T), lambda b, hg: (b, hg, 0)),
            pl.BlockSpec((tpb, T, qk_w), lambda b, hg: (b, 0, hg)),
            pl.BlockSpec((tpb, v_w, T), lambda b, hg: (b, hg, 0)),
            pl.BlockSpec((1, qk_w, BLOCK), lambda b, hg: (meta_tile(b), hg, meta_blk(b))),
            pl.BlockSpec((1, BLOCK, qk_w), lambda b, hg: (meta_tile(b), meta_blk(b), hg)),
            pl.BlockSpec((1, v_w, BLOCK), lambda b, hg: (meta_tile(b), hg, meta_blk(b))),
        ],
        out_specs=(
            pl.BlockSpec((tpb * T, v_w), lambda b, hg: (b, hg)),
            pl.BlockSpec((BLOCK, v_w), lambda b, hg: (b, hg)),
        ),
        scratch_shapes=[pltpu.VMEM((2, nh, T, T), F32),
                        pltpu.VMEM((nh, BLOCK, T), F32)],
        compiler_params=_cparams(("parallel", "parallel")),
        name=name,
    )(qt, k, vt, qt, k, vt)


def _mla_operands(cq, ckv, misc, is_pad, cos, sin, cost, sint, qn_g, kvn_g, wuqt, wuk, wuvt,
                  qt_ref, k_ref, vt_ref):
    half = MLA_ROPE_DIM // 2

    qn = _rms_norm(cq, qn_g).astype(BF16)
    qt = lax.dot_general(wuqt, qn, NT_DIMS, preferred_element_type=F32)
    qt = qt * ((MLA_NOPE_DIM + MLA_ROPE_DIM) ** -0.5 * LOG2E)
    tail =jnp.where(lax.broadcasted_iota(jnp.int32, (LANES - M_BIAS, T), 0) == 0, 1.0, 0.0)
    for h in range(N_HEADS):
        base = h * LANES
        x1 = qt[base + MLA_NOPE_DIM:base + MLA_NOPE_DIM + half, :]
        x2 = qt[base + MLA_NOPE_DIM + half:base + M_BIAS, :]
        qt_ref[0, base:base + MLA_NOPE_DIM, :] = qt[base:base + MLA_NOPE_DIM, :].astype(BF16)
        qt_ref[0, base + MLA_NOPE_DIM:base + MLA_NOPE_DIM + half, :] = (x1 * cost - x2 * sint).astype(BF16)
        qt_ref[0, base + MLA_NOPE_DIM + half:base + M_BIAS, :] = (x2 * cost + x1 * sint).astype(BF16)
        qt_ref[0, base + M_BIAS:base + LANES, :] = tail.astype(BF16)

    kvn = _rms_norm(ckv, kvn_g).astype(BF16)
    k_nope = jnp.dot(kvn, wuk, preferred_element_type=F32)
    lane = lax.broadcasted_iota(jnp.int32, (T, LANES), 1)
    kr = jnp.where((lane >= KR_LANE) & (lane < KR_LANE + MLA_ROPE_DIM), misc, 0.0)
    from_hi = pltpu.roll(kr, LANES - half, 1)
    from_lo = pltpu.roll(kr, half, 1)
    swapped = jnp.where(lane < KR_LANE + half, -from_hi, from_lo)
    k_rot = kr * cos + swapped * sin
    k_rot = jnp.where((lane == M_BIAS) & is_pad, NEG_INF, k_rot)
    k_ref[0] = (k_nope + jnp.concatenate([k_rot] * N_HEADS, axis=1)).astype(BF16)
    vt_ref[0] = lax.dot_general(wuvt, kvn, NT_DIMS, preferred_element_type=F32).astype(BF16)


def _top2_sum(a, b, c, d):
    hi1, lo1 = jnp.maximum(a, b), jnp.minimum(a, b)
    hi2, lo2 = jnp.maximum(c, d), jnp.minimum(c, d)
    return jnp.maximum(hi1, hi2) + jnp.maximum(jnp.minimum(hi1, hi2), jnp.maximum(lo1, lo2))


def _route(logits_t, bias_col):
    scores = jax.nn.sigmoid(logits_t)
    biased = scores + bias_col
    b = [biased[e:e + 1, :] for e in range(N_EXPERTS)]
    s = [scores[e:e + 1, :] for e in range(N_EXPERTS)]
    gscore = [_top2_sum(*b[EXPERTS_PER_GROUP * g:EXPERTS_PER_GROUP * (g + 1)]) for g in range(N_GROUPS)]
    best = gscore[0]
    gidx = jnp.zeros_like(best, dtype=jnp.int32)
    for g in range(1, N_GROUPS):
        better = gscore[g] > best
        gidx = jnp.where(better, g, gidx)
        best = jnp.where(better, gscore[g], best)
    in_g = [gidx == g for g in range(N_GROUPS)]

    def pick(vals, j):
        out = vals[j]
        for g in range(1, N_GROUPS):
            out = jnp.where(in_g[g], vals[EXPERTS_PER_GROUP * g + j], out)
        return out

    vb = [pick(b, j) for j in range(EXPERTS_PER_GROUP)]
    vs = [pick(s, j) for j in range(EXPERTS_PER_GROUP)]
    chosen = []
    for j in range(EXPERTS_PER_GROUP):
        rank = jnp.zeros_like(gidx)
        for i in range(EXPERTS_PER_GROUP):
            if i == j:
                continue
            ahead = (vb[i] >= vb[j]) if i < j else (vb[i] > vb[j])
            rank = rank + jnp.where(ahead, 1, 0)
        chosen.append(rank < 2)
    total = sum(jnp.where(chosen[j], vs[j], 0.0) for j in range(EXPERTS_PER_GROUP))
    gates = [jnp.where(chosen[j], vs[j] / total, 0.0) for j in range(EXPERTS_PER_GROUP)]
    rows = []
    for g in range(N_GROUPS):
        for j in range(EXPERTS_PER_GROUP):
            rows.append(jnp.where(in_g[g], gates[j], 0.0))
    return jnp.concatenate(rows, axis=0), gidx


def _post_kernel(of_ref, ofm_ref, om_ref, omm_ref, g_ref, h_ref, wfo_ref, wmo_ref, wout_ref,
                 lng_ref, lnb_ref, rwh_ref, rwl_ref, rb_ref, h1_ref, comb_ref, gid_ref,
                 *, alpha, n_real_tiles):
    is_meta = pl.program_id(0) >= n_real_tiles
    o_fox = jnp.where(is_meta, ofm_ref[...], of_ref[...])
    o_mla = jnp.where(is_meta, omm_ref[...], om_ref[...])
    y_fox = jnp.dot(o_fox, wfo_ref[...], preferred_element_type=F32)
    y_mla = jnp.dot(o_mla, wmo_ref[...], preferred_element_type=F32)
    merged = (jax.nn.sigmoid(g_ref[:, :D_MODEL].astype(F32)) * y_fox
              + jax.nn.sigmoid(g_ref[:, D_MODEL:].astype(F32)) * y_mla)
    mix = jnp.dot(merged.astype(BF16), wout_ref[...], preferred_element_type=F32)
    h1 = _layer_norm(alpha * h_ref[...] + mix, lng_ref[...], lnb_ref[...])
    h1_ref[...] = h1
    h_hi = h1.astype(BF16)
    h_lo = (h1 - h_hi.astype(F32)).astype(BF16)
    rwh, rwl = rwh_ref[...], rwl_ref[...]
    both = lax.dot_general(jnp.concatenate([rwh, rwl], axis=0), h_hi, NT_DIMS,
                           preferred_element_type=F32)
    logits_t = (both[:N_EXPERTS] + both[N_EXPERTS:]
                + lax.dot_general(rwh, h_lo, NT_DIMS, preferred_element_type=F32))
    comb_ref[...], gid_ref[...] = _route(logits_t, rb_ref[...])


def _post(o_fox, o_mla, g, h, wfo, wmo, wout, lng, lnb, rwh, rwl, rb, alpha):
    rows = h.shape[0]
    tm = TM_POST
    o_fox_r, o_fox_m = o_fox
    o_mla_r, o_mla_m = o_mla
    nrt = o_fox_r.shape[0] // tm
    row = lambda i: (i, 0)
    real = lambda i: (jnp.minimum(i, nrt - 1), 0)
    meta = lambda i: (jnp.maximum(i - nrt, 0), 0)
    full = lambda i: (0, 0)
    return pl.pallas_call(
        functools.partial(_post_kernel, alpha=alpha, n_real_tiles=nrt),
        out_shape=(
            jax.ShapeDtypeStruct((rows, D_MODEL), F32),
            jax.ShapeDtypeStruct((N_EXPERTS, rows), F32),
            jax.ShapeDtypeStruct((1, rows), jnp.int32),
        ),
        grid=(rows // tm,),
        in_specs=[
            pl.BlockSpec((tm, FOX_WIDTH), real),
            pl.BlockSpec((tm, FOX_WIDTH), meta),
            pl.BlockSpec((tm, MLA_WIDTH), real),
            pl.BlockSpec((tm, MLA_WIDTH), meta),
            pl.BlockSpec((tm, 2 * D_MODEL), row),
            pl.BlockSpec((tm, D_MODEL), row),
            pl.BlockSpec((FOX_WIDTH, D_MODEL), full),
            pl.BlockSpec((MLA_WIDTH, D_MODEL), full),
            pl.BlockSpec((D_MODEL, D_MODEL), full),
            pl.BlockSpec((1, D_MODEL), full),
            pl.BlockSpec((1, D_MODEL), full),
            pl.BlockSpec((N_EXPERTS, D_MODEL), full),
            pl.BlockSpec((N_EXPERTS, D_MODEL), full),
            pl.BlockSpec((N_EXPERTS, 1), full),
        ],
        out_specs=(
            pl.BlockSpec((tm, D_MODEL), row),
            pl.BlockSpec((N_EXPERTS, tm), lambda i: (0, i)),
            pl.BlockSpec((1, tm), lambda i: (0, i)),
        ),
        compiler_params=_cparams(("parallel",)),
        name="merge_ln1_router",
    )(o_fox_r, o_fox_m, o_mla_r, o_mla_m, g, h, wfo, wmo, wout, lng, lnb, rwh, rwl, rb)


def _moe_kernel(cnt_ref, h_ref, comb_ref, gid_ref, wg_ref, wu_ref, wd_ref, lng_ref, lnb_ref, o_ref,
                p_ref, pt_ref, xs_ref, cws_ref, ys_ref, *, alpha):
    i = pl.program_id(0)
    g = pl.program_id(1)
    tm, slots = h_ref.shape[0], p_ref.shape[0]

    def chunks(gg):
        return lax.shift_right_logical(cnt_ref[i * N_GROUPS + gg] + (MOE_CHUNK - 1), MOE_CHUNK_LOG2)

    @pl.when(g == 0)
    def _():
        gid = gid_ref[...]
        sub = lax.broadcasted_iota(jnp.int32, (8, tm), 0)
        lane = lax.broadcasted_iota(jnp.int32, (8, tm), 1)
        onehot = jnp.where(gid == sub, 1, 0)
        cum = onehot
        shift = 1
        while shift < tm:
            cum = cum + jnp.where(lane >= shift, pltpu.roll(cum, shift, 1), 0)
            shift *= 2
        slot = jnp.zeros((1, tm), jnp.int32)
        off = jnp.int32(0)
        for gg in range(N_GROUPS):
            slot = slot + onehot[gg:gg + 1, :] * (cum[gg:gg + 1, :] - 1 + off)
            off = off + chunks(gg) * MOE_CHUNK
        srow = lax.broadcasted_iota(jnp.int32, (slots, tm), 0)
        p = jnp.where(srow == slot, 1.0, 0.0).astype(BF16)
        p_ref[...] = p
        stack = jnp.concatenate([comb_ref[...], slot.astype(F32),
                                 jnp.zeros((LANES - N_EXPERTS - 1, tm), F32)], axis=0)
        nat = stack.T
        lane_n = lax.broadcasted_iota(jnp.int32, (tm, LANES), 1)
        cw = jnp.where(lane_n < N_EXPERTS, nat, 0.0)
        slot_col = jnp.sum(jnp.where(lane_n == N_EXPERTS, nat, 0.0), axis=1, keepdims=True)
        scol = lax.broadcasted_iota(jnp.int32, (tm, slots), 1)
        pt_ref[...] = jnp.where(scol == slot_col.astype(jnp.int32), 1.0, 0.0).astype(BF16)
        xs_ref[...] = jnp.dot(p, h_ref[...].astype(BF16), preferred_element_type=F32).astype(BF16)
        cw_hi = cw.astype(BF16)
        cw_lo = (cw - cw_hi.astype(F32)).astype(BF16)
        both = jnp.dot(p, jnp.concatenate([cw_hi, cw_lo], axis=1), preferred_element_type=F32)
        cws_ref[...] = both[:, :LANES] + both[:, LANES:]
        ys_ref[...] = jnp.zeros_like(ys_ref)

    off_g = jnp.int32(0)
    for gg in range(N_GROUPS - 1):
        off_g = off_g + jnp.where(gg < g, chunks(gg), 0) * MOE_CHUNK
    lane_c = lax.broadcasted_iota(jnp.int32, (MOE_CHUNK, LANES), 1)

    def chunk(k, carry):
        r0 = pl.multiple_of(off_g + k * MOE_CHUNK, MOE_CHUNK)
        x = xs_ref[pl.ds(r0, MOE_CHUNK), :]
        cw = cws_ref[pl.ds(r0, MOE_CHUNK), :]
        pieces = []
        for e in range(EXPERTS_PER_GROUP):
            gate = jnp.dot(x, wg_ref[e], preferred_element_type=F32)
            up = jnp.dot(x, wu_ref[e], preferred_element_type=F32)
            hid = gate * jax.nn.sigmoid(gate) * up
            w_e = jnp.sum(jnp.where(lane_c == g * EXPERTS_PER_GROUP + e, cw, 0.0), axis=1, keepdims=True)
            pieces.append((hid * w_e).astype(BF16))
        wd = wd_ref[...].reshape(EXPERTS_PER_GROUP * D_EXPERT, D_MODEL)
        y = jnp.dot(jnp.concatenate(pieces, axis=1), wd, preferred_element_type=F32)
        ys_ref[pl.ds(r0, MOE_CHUNK), :] = y.astype(BF16)
        return carry

    lax.fori_loop(0, chunks(g), chunk, 0)

    @pl.when(g == N_GROUPS - 1)
    def _():
        y = jnp.dot(pt_ref[...], ys_ref[...], preferred_element_type=F32)
        o_ref[...] = _layer_norm(alpha * h_ref[...] + y, lng_ref[...], lnb_ref[...])


def _moe(h1, comb_t, gid, wg, wu, wd, lng, lnb, alpha, n_tiles):
    tm = TM_MOE
    slots = tm + N_GROUPS * MOE_CHUNK
    tile_gid = gid[0, :n_tiles * tm].reshape(n_tiles, tm)
    cnt = jnp.sum(tile_gid[:, :, None] == jnp.arange(N_GROUPS, dtype=jnp.int32), axis=1,
                  dtype=jnp.int32).reshape(-1)
    return pl.pallas_call(
        functools.partial(_moe_kernel, alpha=alpha),
        out_shape=jax.ShapeDtypeStruct((n_tiles * tm, D_MODEL), F32),
        grid_spec=pltpu.PrefetchScalarGridSpec(
            num_scalar_prefetch=1,
            grid=(n_tiles, N_GROUPS),
            in_specs=[
                pl.BlockSpec((tm, D_MODEL), lambda i, g, cnt: (i, 0)),
                pl.BlockSpec((N_EXPERTS, tm), lambda i, g, cnt: (0, i)),
                pl.BlockSpec((1, tm), lambda i, g, cnt: (0, i)),
                pl.BlockSpec((EXPERTS_PER_GROUP, D_MODEL, D_EXPERT), lambda i, g, cnt: (g, 0, 0)),
                pl.BlockSpec((EXPERTS_PER_GROUP, D_MODEL, D_EXPERT), lambda i, g, cnt: (g, 0, 0)),
                pl.BlockSpec((EXPERTS_PER_GROUP, D_EXPERT, D_MODEL), lambda i, g, cnt: (g, 0, 0)),
                pl.BlockSpec((1, D_MODEL), lambda i, g, cnt: (0, 0)),
                pl.BlockSpec((1, D_MODEL), lambda i, g, cnt: (0, 0)),
            ],
            out_specs=pl.BlockSpec((tm, D_MODEL), lambda i, g, cnt: (i, 0)),
            scratch_shapes=[
                pltpu.VMEM((slots, tm), BF16),
                pltpu.VMEM((tm, slots), BF16),
                pltpu.VMEM((slots, D_MODEL), BF16),
                pltpu.VMEM((slots, LANES), F32),
                pltpu.VMEM((slots, D_MODEL), BF16),
            ],
        ),
        compiler_params=_cparams(("parallel", "arbitrary")),
        name="moe_ln2",
    )(cnt, h1, comb_t, gid, wg, wu, wd, lng, lnb)


def _rope_tables(seq):
    half = MLA_ROPE_DIM // 2
    blk = np.arange(T) % BLOCK
    pos = np.concatenate([np.arange(seq) + N_META, np.where(blk < N_META, blk, 0)]).astype(np.float64)
    rows = seq + T
    inv = ROPE_THETA ** (-np.arange(half, dtype=np.float64) / half)
    ang = pos[:, None] * inv[None, :]
    cos, sin = np.cos(ang), np.sin(ang)
    ones = np.ones((rows, KR_LANE))
    tail = LANES - KR_LANE - MLA_ROPE_DIM
    cos_t = np.concatenate([ones, cos, cos, np.ones((rows, tail))], axis=1)
    sin_t = np.concatenate([0 * ones, sin, sin, np.zeros((rows, tail))], axis=1)
    return tuple(jnp.asarray(a, F32) for a in (cos_t, sin_t, cos.T, sin.T))


def _pad_heads(w, n_heads, per_head, keep_lo, keep_hi):
    k = w.shape[0]
    w = w.reshape(k, n_heads, per_head)[:, :, keep_lo:keep_hi]
    w = jnp.pad(w, ((0, 0), (0, 0), (0, LANES - (keep_hi - keep_lo))))
    return w.reshape(k, n_heads * LANES)


def kernel(x, meta_tokens, ln_in_g, ln_in_b, w_in, fox_f_bias, fox_w_o, mla_q_norm, mla_w_uq,
           mla_kv_norm, mla_w_ukv, mla_w_o, w_out, ln1_g, ln1_b, router_w, router_b,
           w_gate, w_up, w_down, ln2_g, ln2_b):
    batch, seq, _ = x.shape
    depth = w_in.shape[0]
    assert seq % T == 0 and batch % META_PER_TILE == 0
    tpb = seq // T
    n_real = batch * tpb
    n_tiles = n_real + batch // META_PER_TILE
    assert (n_real * T) % TM_MOE == 0 and (n_tiles * T) % TM_MOE == 0
    alpha = (2 * depth) ** 0.25
    row = lambda a: a.reshape(1, -1).astype(F32)

    h = x.reshape(batch * seq, D_MODEL)
    ln_in = (meta_tokens.astype(F32), row(ln_in_g), row(ln_in_b))
    tables = _rope_tables(seq)
    pqt, pk = _decay_placement()
    rw_t = router_w.T.astype(F32)
    rw_hi = rw_t.astype(BF16)
    rw_lo = (rw_t - rw_hi.astype(F32)).astype(BF16)
    rb = router_b.reshape(N_EXPERTS, 1).astype(F32)

    o_q = FOX_WIDTH
    o_k = o_q + FOX_WIDTH
    o_v = o_k + FOX_WIDTH
    o_f = o_v + FOX_HEADS
    o_cq = o_f + MLA_Q_RANK
    o_ckv = o_cq + MLA_KV_RANK
    o_kr = o_ckv + MLA_ROPE_DIM
    for i in range(depth):
        w = w_in[i]
        zeros = lambda n: jnp.zeros((D_MODEL, n), w.dtype)
        w_misc = jnp.concatenate([w[:, o_v:o_f], zeros(KR_LANE - FOX_HEADS), w[:, o_ckv:o_kr],
                                  zeros(LANES - KR_LANE - MLA_ROPE_DIM)], axis=1)
        w_k = _pad_heads(w[:, o_q:o_k], FOX_HEADS, FOX_HEAD_DIM, 0, FOX_HEAD_DIM)
        w_big = jnp.concatenate([w_k, w[:, o_f:o_ckv], w[:, o_kr:], w_misc], axis=1).astype(BF16)
        wqt = w[:, :o_q].T.astype(BF16)
        wvt = w[:, o_k:o_v].T.astype(BF16)
        bias_row = jnp.pad(fox_f_bias[i].astype(F32), (0, LANES - FOX_HEADS)).reshape(1, LANES)
        qk_dim = MLA_NOPE_DIM + MLA_ROPE_DIM
        wuqt = _pad_heads(mla_w_uq[i], MLA_HEADS, qk_dim, 0, qk_dim).T.astype(BF16)
        kv_dim = MLA_NOPE_DIM + MLA_V_DIM
        wuk = _pad_heads(mla_w_ukv[i], MLA_HEADS, kv_dim, 0, MLA_NOPE_DIM).astype(BF16)
        wuv = mla_w_ukv[i].reshape(MLA_KV_RANK, MLA_HEADS, kv_dim)[:, :, MLA_NOPE_DIM:]
        wuvt = wuv.reshape(MLA_KV_RANK, MLA_WIDTH).T.astype(BF16)
        outs = _proj(h, wqt, wvt, w_big, pqt, pk, bias_row, tables, row(mla_q_norm[i]),
                     row(mla_kv_norm[i]), wuqt, wuk, wuvt, n_real, tpb, n_tiles,
                     ln_in=ln_in if i == 0 else None)
        qt_f, k_f, vt_f, g, qt_m, k_m, vt_m = outs[:7]
        if i == 0:
            h = outs[7]
        o_fox = _attention(qt_f, k_f, vt_f, batch, n_real, tpb, "fox_attn")
        o_mla = _attention(qt_m, k_m, vt_m, batch, n_real, tpb, "mla_attn")

        h1, comb_t, gid = _post(o_fox, o_mla, g, h, fox_w_o[i].astype(BF16), mla_w_o[i].astype(BF16),
                                w_out[i].astype(BF16), row(ln1_g[i]), row(ln1_b[i]), rw_hi, rw_lo, rb,
                                alpha)

        wg, wu, wd = w_gate[i].astype(BF16), w_up[i].astype(BF16), w_down[i].astype(BF16)
        moe_rows = (n_real if i == depth - 1 else n_tiles) * T
        h = _moe(h1, comb_t, gid, wg, wu, wd, row(ln2_g[i]), row(ln2_b[i]), alpha, moe_rows // TM_MOE)

    return h.reshape(batch, seq, D_MODEL)
```

```python
import functools
import math

import jax
import jax.numpy as jnp
import numpy as np
from jax import lax
from jax.experimental import pallas as pl
from jax.experimental.pallas import tpu as pltpu

F32 = jnp.float32
BF16 = jnp.bfloat16

D_MODEL = 1024
N_META = 16
BLOCK = 128
NEG_INF = -1e30
LOG2E = math.log2(math.e)

FOX_HEADS = 8
FOX_HEAD_DIM = 64
FOX_WIDTH = FOX_HEADS * FOX_HEAD_DIM

MLA_HEADS = 8
MLA_NOPE_DIM = 64
MLA_ROPE_DIM = 32
MLA_V_DIM = 64
MLA_Q_RANK = 384
MLA_KV_RANK = 256
MLA_WIDTH = MLA_HEADS * MLA_V_DIM
ROPE_THETA = 10000.0

N_EXPERTS = 16
N_GROUPS = 4
EXPERTS_PER_GROUP = N_EXPERTS // N_GROUPS
D_EXPERT = 256

LN_EPS = 1e-5
RMS_EPS = 1e-6

LANES = 128
N_HEADS = FOX_HEADS
ATTN_HEADS = 4
WIDE = N_HEADS * LANES
KR_LANE = 64

E_HI, E_MID, E_LO, E_ONE, E_PAD = 0, N_HEADS, 2 * N_HEADS, 3 * N_HEADS, 3 * N_HEADS + 1
X_CQ, X_ONE_K, X_BIAS = FOX_HEAD_DIM, FOX_HEAD_DIM + 3, FOX_HEAD_DIM + 6
X_ROWS = 8
M_BIAS = MLA_NOPE_DIM + MLA_ROPE_DIM

C_K = 0
C_CQ = WIDE
C_CKV = C_CQ + MLA_Q_RANK
C_G = C_CKV + MLA_KV_RANK
C_MISC = C_G + 2 * D_MODEL
PROJ_COLS = C_MISC + LANES

VMEM_LIMIT = 56 * 1024 * 1024

T = 512
META_PER_TILE = T // BLOCK
TM_POST = 512
TM_MOE = 1024
MOE_CHUNK_LOG2 = 7
MOE_CHUNK = 1 << MOE_CHUNK_LOG2

NT_DIMS = (((1,), (1,)), ((), ()))


def _cparams(sem):
    return pltpu.CompilerParams(dimension_semantics=sem, vmem_limit_bytes=VMEM_LIMIT)


def _layer_norm(x, g, b):
    mu = jnp.mean(x, axis=-1, keepdims=True)
    xc = x - mu
    var = jnp.mean(xc * xc, axis=-1, keepdims=True)
    return xc * lax.rsqrt(var + LN_EPS) * g + b


def _rms_norm(x, g):
    ms = jnp.mean(x * x, axis=-1, keepdims=True)
    return x * lax.rsqrt(ms + RMS_EPS) * g


def _proj_kernel(*refs, n_real, tpb, first):
    if first:
        x_ref, meta_ref, lng_ref, lnb_ref = refs[:4]
        refs = refs[4:]
    else:
        x_ref, refs = refs[0], refs[1:]
    (wqt_ref, wvt_ref, w_ref, pqt_ref, pk_ref, bias_ref, cos_ref, sin_ref, cost_ref, sint_ref,
     qn_ref, kvn_ref, wuqt_ref, wuk_ref, wuvt_ref,
     qt_ref, k_ref, vt_ref, g_ref, qtm_ref, km_ref, vtm_ref) = refs[:22]
    carry_ref, c_ref = refs[-2:]
    i = pl.program_id(0)
    if first:
        h_ref = refs[22]

        @pl.when(i < n_real)
        def _():
            h_ref[...] = _layer_norm(x_ref[...], lng_ref[...], lnb_ref[...])

        @pl.when(i >= n_real)
        def _():
            h_ref[...] = jnp.zeros_like(h_ref)
            m = _layer_norm(meta_ref[...], lng_ref[...], lnb_ref[...])
            for jb in range(META_PER_TILE):
                h_ref[jb * BLOCK:jb * BLOCK + N_META, :] = m

        x = h_ref[...].astype(BF16)
    else:
        x = x_ref[...].astype(BF16)

    def mm(lo, hi):
        return jnp.dot(x, w_ref[:, lo:hi], preferred_element_type=F32)

    misc = mm(C_MISC, PROJ_COLS)
    g_ref[:, :D_MODEL] = mm(C_G, C_G + D_MODEL).astype(BF16)
    g_ref[:, D_MODEL:] = mm(C_G + D_MODEL, C_MISC).astype(BF16)

    z = misc + bias_ref[...]
    logf = jnp.minimum(z, 0.0) - jnp.log1p(jnp.exp(-jnp.abs(z)))
    row = lax.broadcasted_iota(jnp.int32, (T, LANES), 0)
    lane = lax.broadcasted_iota(jnp.int32, (T, LANES), 1)
    blk_row = row % BLOCK
    is_meta = i >= n_real

    @pl.when(jnp.logical_not(is_meta))
    def _():
        @pl.when(i % tpb == 0)
        def _():
            carry_ref[...] = jnp.zeros_like(carry_ref)

        c = logf
        shift = 1
        while shift < T:
            c = c + jnp.where(row >= shift, pltpu.roll(c, shift, 0), 0.0)
            shift *= 2
        c = c + carry_ref[...]
        carry_ref[...] = c[T - 1:T, :]
        c_ref[...] = c

    @pl.when(is_meta)
    def _():
        own = jnp.where(blk_row < N_META, logf, 0.0)
        s = own
        shift = 1
        while shift < BLOCK:
            s = s + jnp.where(blk_row < BLOCK - shift, pltpu.roll(s, T - shift, 0), 0.0)
            shift *= 2
        c_ref[...] = own - s

    is_pad = is_meta & (blk_row >= N_META)
    _mla_operands(mm(C_CQ, C_CKV), mm(C_CKV, C_G), misc, is_pad, cos_ref[...], sin_ref[...],
                  cost_ref[...], sint_ref[...], qn_ref[...], kvn_ref[...], wuqt_ref[...],
                  wuk_ref[...], wuvt_ref[...], qtm_ref, km_ref, vtm_ref)
    c2 = c_ref[...] * LOG2E
    hi = c2.astype(BF16).astype(F32)
    r1 = c2 - hi
    mid = r1.astype(BF16).astype(F32)
    lo = (r1 - mid).astype(BF16).astype(F32)
    feat = jnp.where(lane < E_MID, hi,
           jnp.where(lane < E_LO, pltpu.roll(mid, E_MID, 1),
           jnp.where(lane < E_ONE, pltpu.roll(lo, E_LO, 1),
           jnp.where(lane == E_ONE, 1.0,
           jnp.where((lane == E_PAD) & is_pad, 1.0, 0.0)))))
    feat = feat.astype(BF16)
    extra_k = jnp.dot(feat, pk_ref[...], preferred_element_type=F32)
    extra_qt = lax.dot_general(pqt_ref[...], feat, NT_DIMS, preferred_element_type=F32)
    no_extra = jnp.zeros((LANES - FOX_HEAD_DIM - X_ROWS, T), F32)

    k_ref[0] = (mm(C_K, C_CQ) + extra_k).astype(BF16)
    qt = lax.dot_general(wqt_ref[...], x, NT_DIMS, preferred_element_type=F32)
    qt = qt * (FOX_HEAD_DIM ** -0.5 * LOG2E)
    for h in range(N_HEADS):
        qt_ref[0, h * LANES:h * LANES + FOX_HEAD_DIM, :] = (
            qt[h * FOX_HEAD_DIM:(h + 1) * FOX_HEAD_DIM, :].astype(BF16))
        qt_ref[0, h * LANES + FOX_HEAD_DIM:(h + 1) * LANES, :] = jnp.concatenate(
            [extra_qt[h * X_ROWS:(h + 1) * X_ROWS, :], no_extra], axis=0).astype(BF16)
    vt_ref[0] = lax.dot_general(wvt_ref[...], x, NT_DIMS, preferred_element_type=F32).astype(BF16)


def _proj(h, wqt, wvt, w_big, pqt, pk, bias_row, tables, qn, kvn, wuqt, wuk, wuvt, n_real, tpb,
          n_tiles, ln_in=None):
    first = ln_in is not None
    rows = n_tiles * T
    cos_t, sin_t, cos_tt, sin_tt = tables
    half = MLA_ROPE_DIM // 2
    row = lambda i: (i, 0)
    full = lambda i: (0, 0)
    blk = lambda i: (i, 0, 0)
    tab = lambda i: jnp.where(i < n_real, i % tpb, tpb)
    operands = (
        jax.ShapeDtypeStruct((n_tiles, WIDE, T), BF16),
        jax.ShapeDtypeStruct((n_tiles, T, WIDE), BF16),
        jax.ShapeDtypeStruct((n_tiles, N_HEADS * MLA_V_DIM, T), BF16),
    )
    operand_specs = (
        pl.BlockSpec((1, WIDE, T), blk),
        pl.BlockSpec((1, T, WIDE), blk),
        pl.BlockSpec((1, N_HEADS * MLA_V_DIM, T), blk),
    )
    out_shape = operands + (jax.ShapeDtypeStruct((rows, 2 * D_MODEL), BF16),) + operands
    out_specs = operand_specs + (pl.BlockSpec((T, 2 * D_MODEL), row),) + operand_specs
    if first:
        x_specs = [
            pl.BlockSpec((T, D_MODEL), lambda i: (jnp.minimum(i, n_real - 1), 0)),
            pl.BlockSpec((N_META, D_MODEL), full),
            pl.BlockSpec((1, D_MODEL), full),
            pl.BlockSpec((1, D_MODEL), full),
        ]
        x_args = (h,) + tuple(ln_in)
        out_shape += (jax.ShapeDtypeStruct((rows, D_MODEL), F32),)
        out_specs += (pl.BlockSpec((T, D_MODEL), row),)
    else:
        x_specs = [pl.BlockSpec((T, D_MODEL), row)]
        x_args = (h,)
    return pl.pallas_call(
        functools.partial(_proj_kernel, n_real=n_real, tpb=tpb, first=first),
        out_shape=out_shape,
        grid=(n_tiles,),
        in_specs=x_specs + [
            pl.BlockSpec((FOX_WIDTH, D_MODEL), full),
            pl.BlockSpec((FOX_WIDTH, D_MODEL), full),
            pl.BlockSpec((D_MODEL, PROJ_COLS), full),
            pl.BlockSpec((N_HEADS * X_ROWS, LANES), full),
            pl.BlockSpec((LANES, WIDE), full),
            pl.BlockSpec((1, LANES), full),
            pl.BlockSpec((T, LANES), lambda i: (tab(i), 0)),
            pl.BlockSpec((T, LANES), lambda i: (tab(i), 0)),
            pl.BlockSpec((half, T), lambda i: (0, tab(i))),
            pl.BlockSpec((half, T), lambda i: (0, tab(i))),
            pl.BlockSpec((1, MLA_Q_RANK), full),
            pl.BlockSpec((1, MLA_KV_RANK), full),
            pl.BlockSpec((WIDE, MLA_Q_RANK), full),
            pl.BlockSpec((MLA_KV_RANK, WIDE), full),
            pl.BlockSpec((MLA_WIDTH, MLA_KV_RANK), full),
        ],
        out_specs=out_specs,
        scratch_shapes=[pltpu.VMEM((1, LANES), F32), pltpu.VMEM((T, LANES), F32)],
        compiler_params=_cparams(("arbitrary",)),
        name="in_proj",
    )(*x_args, wqt, wvt, w_big, pqt, pk, bias_row, cos_t, sin_t, cos_tt, sin_tt, qn, kvn, wuqt, wuk, wuvt)


def _decay_placement():
    pk = [[0.0] * WIDE for _ in range(LANES)]
    pqt = [[0.0] * LANES for _ in range(N_HEADS * X_ROWS)]
    for h in range(N_HEADS):
        base = h * LANES
        qbase = h * X_ROWS - X_CQ
        for s, e in enumerate((E_HI, E_MID, E_LO)):
            pqt[qbase + X_CQ + s][e + h] = 1.0
            pk[E_ONE][base + X_CQ + s] = 1.0
            pqt[qbase + X_ONE_K + s][E_ONE] = 1.0
            pk[e + h][base + X_ONE_K + s] = -1.0
        pqt[qbase + X_BIAS][E_ONE] = 1.0
        pk[E_PAD][base + X_BIAS] = NEG_INF
    return jnp.array(pqt, F32).astype(BF16), jnp.array(pk, F32).astype(BF16)


def _attn_kernel(qt_ref, k_ref, vt_ref, qtm_ref, km_ref, vtm_ref, o_ref, om_ref, st_ref, stm_ref):
    nh = ATTN_HEADS
    n_blocks = qt_ref.shape[0]
    km = km_ref[0]
    vtm = vtm_ref[0]

    def causal(n):
        return (lax.broadcasted_iota(jnp.int32, (n, n), 0)
                <= lax.broadcasted_iota(jnp.int32, (n, n), 1))

    def head(a, jj, width):
        return a[jj * width:(jj + 1) * width]

    qtm = qtm_ref[0]
    outs = []
    for jj in range(nh):
        st = jnp.dot(km[:, jj * LANES:(jj + 1) * LANES], head(qtm, jj, LANES),
                     preferred_element_type=F32)
        st = jnp.where(causal(BLOCK), st, NEG_INF)
        p = jnp.exp2(st - jnp.max(st, axis=0, keepdims=True))
        pv = jnp.dot(head(vtm, jj, MLA_V_DIM), p.astype(BF16), preferred_element_type=F32)
        outs.append(pv / jnp.sum(p, axis=0, keepdims=True))
    om_ref[...] = jnp.concatenate(outs, axis=0).T.astype(om_ref.dtype)

    def query_block(qi, carry):
        qt = qt_ref[qi]

        def scores(kj, slot):
            k = k_ref[kj]
            for jj in range(nh):
                st_ref[slot, jj] = jnp.dot(k[:, jj * LANES:(jj + 1) * LANES], head(qt, jj, LANES),
                                           preferred_element_type=F32)

        def consume(kj, slot, state, diagonal):
            vt = vt_ref[kj]
            out = []
            for jj in range(nh):
                m, l, acc = state[jj]
                st = st_ref[slot, jj]
                m_new = m
                if diagonal:
                    st = jnp.where(causal(T), st, NEG_INF)
                    stm = stm_ref[jj]
                    m_new = jnp.maximum(m_new, jnp.max(stm, axis=0, keepdims=True))
                m_new = jnp.maximum(m_new, jnp.max(st, axis=0, keepdims=True))
                alpha = jnp.exp2(m - m_new)
                p = jnp.exp2(st - m_new)
                l_new = alpha * l + jnp.sum(p, axis=0, keepdims=True)
                pv = jnp.dot(head(vt, jj, MLA_V_DIM), p.astype(BF16), preferred_element_type=F32)
                if diagonal:
                    pm = jnp.exp2(stm - m_new)
                    l_new = l_new + jnp.sum(pm, axis=0, keepdims=True)
                    pv = pv + jnp.dot(head(vtm, jj, MLA_V_DIM), pm.astype(BF16),
                                      preferred_element_type=F32)
                out.append((m_new, l_new, alpha * acc + pv))
            return tuple(out)

        init_one = (jnp.full((1, T), NEG_INF, F32), jnp.zeros((1, T), F32),
                    jnp.zeros((MLA_V_DIM, T), F32))
        state = (init_one,) * nh

        def pair(i, state):
            c0 = 2 * i
            scores(c0 + 1, 1)
            state = consume(c0, 0, state, False)
            scores(c0 + 2, 0)
            return consume(c0 + 1, 1, state, False)

        def odd_tail(state):
            scores(qi, 1)
            state = consume(qi - 1, 0, state, False)
            return consume(qi, 1, state, True)

        def even_tail(state):
            return consume(qi, 0, state, True)

        scores(0, 0)
        for jj in range(nh):
            stm_ref[jj] = jnp.dot(km[:, jj * LANES:(jj + 1) * LANES], head(qt, jj, LANES),
                                  preferred_element_type=F32)
        state = lax.fori_loop(0, qi // 2, pair, state)
        state = lax.cond(qi % 2 == 1, odd_tail, even_tail, state)
        ot = jnp.concatenate([acc / l for (_, l, acc) in state], axis=0)
        o_ref[pl.ds(pl.multiple_of(qi * T, T), T), :] = ot.T.astype(o_ref.dtype)
        return carry

    lax.fori_loop(0, n_blocks, query_block, 0)


def _attention(qt, k, vt, batch, n_real, tpb, name):
    nh = ATTN_HEADS
    qk_w, v_w = nh * LANES, nh * MLA_V_DIM
    meta_tile = lambda b: n_real + b // META_PER_TILE
    meta_blk = lambda b: b % META_PER_TILE
    return pl.pallas_call(
        _attn_kernel,
        out_shape=(
            jax.ShapeDtypeStruct((n_real * T, N_HEADS * MLA_V_DIM), BF16),
            jax.ShapeDtypeStruct((batch * BLOCK, N_HEADS * MLA_V_DIM), BF16),
        ),
        grid=(batch, N_HEADS // nh),
        in_specs=[
            pl.BlockSpec((tpb, qk_w, T), lambda b, hg: (b, hg, 0)),
            pl.BlockSpec((tpb, T, qk_w), lambda b, hg: (b, 0, hg)),
            pl.BlockSpec((tpb, v_w, T), lambda b, hg: (b, hg, 0)),
            pl.BlockSpec((1, qk_w, BLOCK), lambda b, hg: (meta_tile(b), hg, meta_blk(b))),
            pl.BlockSpec((1, BLOCK, qk_w), lambda b, hg: (meta_tile(b), meta_blk(b), hg)),
            pl.BlockSpec((1, v_w, BLOCK), lambda b, hg: (meta_tile(b), hg, meta_blk(b))),
        ],
        out_specs=(
            pl.BlockSpec((tpb * T, v_w), lambda b, hg: (b, hg)),
            pl.BlockSpec((BLOCK, v_w), lambda b, hg: (b, hg)),
        ),
        scratch_shapes=[pltpu.VMEM((2, nh, T, T), F32),
                        pltpu.VMEM((nh, BLOCK, T), F32)],
        compiler_params=_cparams(("parallel", "parallel")),
        name=name,
    )(qt, k, vt, qt, k, vt)


def _mla_operands(cq, ckv, misc, is_pad, cos, sin, cost, sint, qn_g, kvn_g, wuqt, wuk, wuvt,
                  qt_ref, k_ref, vt_ref):
    half = MLA_ROPE_DIM // 2

    qn = _rms_norm(cq, qn_g).astype(BF16)
    qt = lax.dot_general(wuqt, qn, NT_DIMS, preferred_element_type=F32)
    qt = qt * ((MLA_NOPE_DIM + MLA_ROPE_DIM) ** -0.5 * LOG2E)
    tail =jnp.where(lax.broadcasted_iota(jnp.int32, (LANES - M_BIAS, T), 0) == 0, 1.0, 0.0)
    for h in range(N_HEADS):
        base = h * LANES
        x1 = qt[base + MLA_NOPE_DIM:base + MLA_NOPE_DIM + half, :]
        x2 = qt[base + MLA_NOPE_DIM + half:base + M_BIAS, :]
        qt_ref[0, base:base + MLA_NOPE_DIM, :] = qt[base:base + MLA_NOPE_DIM, :].astype(BF16)
        qt_ref[0, base + MLA_NOPE_DIM:base + MLA_NOPE_DIM + half, :] = (x1 * cost - x2 * sint).astype(BF16)
        qt_ref[0, base + MLA_NOPE_DIM + half:base + M_BIAS, :] = (x2 * cost + x1 * sint).astype(BF16)
        qt_ref[0, base + M_BIAS:base + LANES, :] = tail.astype(BF16)

    kvn = _rms_norm(ckv, kvn_g).astype(BF16)
    k_nope = jnp.dot(kvn, wuk, preferred_element_type=F32)
    lane = lax.broadcasted_iota(jnp.int32, (T, LANES), 1)
    kr = jnp.where((lane >= KR_LANE) & (lane < KR_LANE + MLA_ROPE_DIM), misc, 0.0)
    from_hi = pltpu.roll(kr, LANES - half, 1)
    from_lo = pltpu.roll(kr, half, 1)
    swapped = jnp.where(lane < KR_LANE + half, -from_hi, from_lo)
    k_rot = kr * cos + swapped * sin
    k_rot = jnp.where((lane == M_BIAS) & is_pad, NEG_INF, k_rot)
    k_ref[0] = (k_nope + jnp.concatenate([k_rot] * N_HEADS, axis=1)).astype(BF16)
    vt_ref[0] = lax.dot_general(wuvt, kvn, NT_DIMS, preferred_element_type=F32).astype(BF16)


def _top2_sum(a, b, c, d):
    hi1, lo1 = jnp.maximum(a, b), jnp.minimum(a, b)
    hi2, lo2 = jnp.maximum(c, d), jnp.minimum(c, d)
    return jnp.maximum(hi1, hi2) + jnp.maximum(jnp.minimum(hi1, hi2), jnp.maximum(lo1, lo2))


def _route(logits_t, bias_col):
    scores = jax.nn.sigmoid(logits_t)
    biased = scores + bias_col
    b = [biased[e:e + 1, :] for e in range(N_EXPERTS)]
    s = [scores[e:e + 1, :] for e in range(N_EXPERTS)]
    gscore = [_top2_sum(*b[EXPERTS_PER_GROUP * g:EXPERTS_PER_GROUP * (g + 1)]) for g in range(N_GROUPS)]
    best = gscore[0]
    gidx = jnp.zeros_like(best, dtype=jnp.int32)
    for g in range(1, N_GROUPS):
        better = gscore[g] > best
        gidx = jnp.where(better, g, gidx)
        best = jnp.where(better, gscore[g], best)
    in_g = [gidx == g for g in range(N_GROUPS)]

    def pick(vals, j):
        out = vals[j]
        for g in range(1, N_GROUPS):
            out = jnp.where(in_g[g], vals[EXPERTS_PER_GROUP * g + j], out)
        return out

    vb = [pick(b, j) for j in range(EXPERTS_PER_GROUP)]
    vs = [pick(s, j) for j in range(EXPERTS_PER_GROUP)]
    chosen = []
    for j in range(EXPERTS_PER_GROUP):
        rank = jnp.zeros_like(gidx)
        for i in range(EXPERTS_PER_GROUP):
            if i == j:
                continue
            ahead = (vb[i] >= vb[j]) if i < j else (vb[i] > vb[j])
            rank = rank + jnp.where(ahead, 1, 0)
        chosen.append(rank < 2)
    total = sum(jnp.where(chosen[j], vs[j], 0.0) for j in range(EXPERTS_PER_GROUP))
    gates = [jnp.where(chosen[j], vs[j] / total, 0.0) for j in range(EXPERTS_PER_GROUP)]
    rows = []
    for g in range(N_GROUPS):
        for j in range(EXPERTS_PER_GROUP):
            rows.append(jnp.where(in_g[g], gates[j], 0.0))
    return jnp.concatenate(rows, axis=0), gidx


def _post_kernel(of_ref, ofm_ref, om_ref, omm_ref, g_ref, h_ref, wfo_ref, wmo_ref, wout_ref,
                 lng_ref, lnb_ref, rwh_ref, rwl_ref, rb_ref, h1_ref, comb_ref, gid_ref,
                 *, alpha, n_real_tiles):
    is_meta = pl.program_id(0) >= n_real_tiles
    o_fox = jnp.where(is_meta, ofm_ref[...], of_ref[...])
    o_mla = jnp.where(is_meta, omm_ref[...], om_ref[...])
    y_fox = jnp.dot(o_fox, wfo_ref[...], preferred_element_type=F32)
    y_mla = jnp.dot(o_mla, wmo_ref[...], preferred_element_type=F32)
    merged = (jax.nn.sigmoid(g_ref[:, :D_MODEL].astype(F32)) * y_fox
              + jax.nn.sigmoid(g_ref[:, D_MODEL:].astype(F32)) * y_mla)
    mix = jnp.dot(merged.astype(BF16), wout_ref[...], preferred_element_type=F32)
    h1 = _layer_norm(alpha * h_ref[...] + mix, lng_ref[...], lnb_ref[...])
    h1_ref[...] = h1
    h_hi = h1.astype(BF16)
    h_lo = (h1 - h_hi.astype(F32)).astype(BF16)
    rwh, rwl = rwh_ref[...], rwl_ref[...]
    both = lax.dot_general(jnp.concatenate([rwh, rwl], axis=0), h_hi, NT_DIMS,
                           preferred_element_type=F32)
    logits_t = (both[:N_EXPERTS] + both[N_EXPERTS:]
                + lax.dot_general(rwh, h_lo, NT_DIMS, preferred_element_type=F32))
    comb_ref[...], gid_ref[...] = _route(logits_t, rb_ref[...])


def _post(o_fox, o_mla, g, h, wfo, wmo, wout, lng, lnb, rwh, rwl, rb, alpha):
    rows = h.shape[0]
    tm = TM_POST
    o_fox_r, o_fox_m = o_fox
    o_mla_r, o_mla_m = o_mla
    nrt = o_fox_r.shape[0] // tm
    row = lambda i: (i, 0)
    real = lambda i: (jnp.minimum(i, nrt - 1), 0)
    meta = lambda i: (jnp.maximum(i - nrt, 0), 0)
    full = lambda i: (0, 0)
    return pl.pallas_call(
        functools.partial(_post_kernel, alpha=alpha, n_real_tiles=nrt),
        out_shape=(
            jax.ShapeDtypeStruct((rows, D_MODEL), F32),
            jax.ShapeDtypeStruct((N_EXPERTS, rows), F32),
            jax.ShapeDtypeStruct((1, rows), jnp.int32),
        ),
        grid=(rows // tm,),
        in_specs=[
            pl.BlockSpec((tm, FOX_WIDTH), real),
            pl.BlockSpec((tm, FOX_WIDTH), meta),
            pl.BlockSpec((tm, MLA_WIDTH), real),
            pl.BlockSpec((tm, MLA_WIDTH), meta),
            pl.BlockSpec((tm, 2 * D_MODEL), row),
            pl.BlockSpec((tm, D_MODEL), row),
            pl.BlockSpec((FOX_WIDTH, D_MODEL), full),
            pl.BlockSpec((MLA_WIDTH, D_MODEL), full),
            pl.BlockSpec((D_MODEL, D_MODEL), full),
            pl.BlockSpec((1, D_MODEL), full),
            pl.BlockSpec((1, D_MODEL), full),
            pl.BlockSpec((N_EXPERTS, D_MODEL), full),
            pl.BlockSpec((N_EXPERTS, D_MODEL), full),
            pl.BlockSpec((N_EXPERTS, 1), full),
        ],
        out_specs=(
            pl.BlockSpec((tm, D_MODEL), row),
            pl.BlockSpec((N_EXPERTS, tm), lambda i: (0, i)),
            pl.BlockSpec((1, tm), lambda i: (0, i)),
        ),
        compiler_params=_cparams(("parallel",)),
        name="merge_ln1_router",
    )(o_fox_r, o_fox_m, o_mla_r, o_mla_m, g, h, wfo, wmo, wout, lng, lnb, rwh, rwl, rb)


def _moe_kernel(cnt_ref, h_ref, comb_ref, gid_ref, wg_ref, wu_ref, wd_ref, lng_ref, lnb_ref, o_ref,
                p_ref, xs_ref, cws_ref, ys_ref, *, alpha):
    i = pl.program_id(0)
    g = pl.program_id(1)
    tm, slots = h_ref.shape[0], p_ref.shape[0]

    def chunks(gg):
        return lax.shift_right_logical(cnt_ref[i * N_GROUPS + gg] + (MOE_CHUNK - 1), MOE_CHUNK_LOG2)

    @pl.when(g == 0)
    def _():
        gid = gid_ref[...]
        sub = lax.broadcasted_iota(jnp.int32, (8, tm), 0)
        lane = lax.broadcasted_iota(jnp.int32, (8, tm), 1)
        onehot = jnp.where(gid == sub, 1, 0)
        cum = onehot
        shift = 1
        while shift < tm:
            cum = cum + jnp.where(lane >= shift, pltpu.roll(cum, shift, 1), 0)
            shift *= 2
        slot = jnp.zeros((1, tm), jnp.int32)
        off = jnp.int32(0)
        for gg in range(N_GROUPS):
            slot = slot + onehot[gg:gg + 1, :] * (cum[gg:gg + 1, :] - 1 + off)
            off = off + chunks(gg) * MOE_CHUNK
        srow = lax.broadcasted_iota(jnp.int32, (slots, tm), 0)
        p = jnp.where(srow == slot, 1.0, 0.0).astype(BF16)
        p_ref[...] = p
        stack = jnp.concatenate([comb_ref[...], jnp.zeros((LANES - N_EXPERTS, tm), F32)], axis=0)
        cw = stack.T
        xs_ref[...] = jnp.dot(p, h_ref[...].astype(BF16), preferred_element_type=F32).astype(BF16)
        cw_hi = cw.astype(BF16)
        cw_lo = (cw - cw_hi.astype(F32)).astype(BF16)
        both = jnp.dot(p, jnp.concatenate([cw_hi, cw_lo], axis=1), preferred_element_type=F32)
        cws_ref[...] = both[:, :LANES] + both[:, LANES:]
        ys_ref[...] = jnp.zeros_like(ys_ref)

    off_g = jnp.int32(0)
    for gg in range(N_GROUPS - 1):
        off_g = off_g + jnp.where(gg < g, chunks(gg), 0) * MOE_CHUNK
    lane_c = lax.broadcasted_iota(jnp.int32, (MOE_CHUNK, LANES), 1)

    def chunk(k, carry):
        r0 = pl.multiple_of(off_g + k * MOE_CHUNK, MOE_CHUNK)
        x = xs_ref[pl.ds(r0, MOE_CHUNK), :]
        cw = cws_ref[pl.ds(r0, MOE_CHUNK), :]
        pieces = []
        for e in range(EXPERTS_PER_GROUP):
            gate = jnp.dot(x, wg_ref[e], preferred_element_type=F32)
            up = jnp.dot(x, wu_ref[e], preferred_element_type=F32)
            hid = gate * jax.nn.sigmoid(gate) * up
            w_e = jnp.sum(jnp.where(lane_c == g * EXPERTS_PER_GROUP + e, cw, 0.0), axis=1, keepdims=True)
            pieces.append((hid * w_e).astype(BF16))
        wd = wd_ref[...].reshape(EXPERTS_PER_GROUP * D_EXPERT, D_MODEL)
        y = jnp.dot(jnp.concatenate(pieces, axis=1), wd, preferred_element_type=F32)
        ys_ref[pl.ds(r0, MOE_CHUNK), :] = y.astype(BF16)
        return carry

    lax.fori_loop(0, chunks(g), chunk, 0)

    @pl.when(g == N_GROUPS - 1)
    def _():
        y = lax.dot_general(p_ref[...], ys_ref[...], (((0,), (0,)), ((), ())),
                            preferred_element_type=F32)
        o_ref[...] = _layer_norm(alpha * h_ref[...] + y, lng_ref[...], lnb_ref[...])


def _moe(h1, comb_t, gid, wg, wu, wd, lng, lnb, alpha, n_tiles):
    tm = TM_MOE
    slots = tm + N_GROUPS * MOE_CHUNK
    tile_gid = gid[0, :n_tiles * tm].reshape(n_tiles, tm)
    cnt = jnp.sum(tile_gid[:, :, None] == jnp.arange(N_GROUPS, dtype=jnp.int32), axis=1,
                  dtype=jnp.int32).reshape(-1)
    return pl.pallas_call(
        functools.partial(_moe_kernel, alpha=alpha),
        out_shape=jax.ShapeDtypeStruct((n_tiles * tm, D_MODEL), F32),
        grid_spec=pltpu.PrefetchScalarGridSpec(
            num_scalar_prefetch=1,
            grid=(n_tiles, N_GROUPS),
            in_specs=[
                pl.BlockSpec((tm, D_MODEL), lambda i, g, cnt: (i, 0)),
                pl.BlockSpec((N_EXPERTS, tm), lambda i, g, cnt: (0, i)),
                pl.BlockSpec((1, tm), lambda i, g, cnt: (0, i)),
                pl.BlockSpec((EXPERTS_PER_GROUP, D_MODEL, D_EXPERT), lambda i, g, cnt: (g, 0, 0)),
                pl.BlockSpec((EXPERTS_PER_GROUP, D_MODEL, D_EXPERT), lambda i, g, cnt: (g, 0, 0)),
                pl.BlockSpec((EXPERTS_PER_GROUP, D_EXPERT, D_MODEL), lambda i, g, cnt: (g, 0, 0)),
                pl.BlockSpec((1, D_MODEL), lambda i, g, cnt: (0, 0)),
                pl.BlockSpec((1, D_MODEL), lambda i, g, cnt: (0, 0)),
            ],
            out_specs=pl.BlockSpec((tm, D_MODEL), lambda i, g, cnt: (i, 0)),
            scratch_shapes=[
                pltpu.VMEM((slots, tm), BF16),
                pltpu.VMEM((slots, D_MODEL), BF16),
                pltpu.VMEM((slots, LANES), F32),
                pltpu.VMEM((slots, D_MODEL), BF16),
            ],
        ),
        compiler_params=_cparams(("parallel", "arbitrary")),
        name="moe_ln2",
    )(cnt, h1, comb_t, gid, wg, wu, wd, lng, lnb)


def _rope_tables(seq):
    half = MLA_ROPE_DIM // 2
    blk = np.arange(T) % BLOCK
    pos = np.concatenate([np.arange(seq) + N_META, np.where(blk < N_META, blk, 0)]).astype(np.float64)
    rows = seq + T
    inv = ROPE_THETA ** (-np.arange(half, dtype=np.float64) / half)
    ang = pos[:, None] * inv[None, :]
    cos, sin = np.cos(ang), np.sin(ang)
    ones = np.ones((rows, KR_LANE))
    tail = LANES - KR_LANE - MLA_ROPE_DIM
    cos_t = np.concatenate([ones, cos, cos, np.ones((rows, tail))], axis=1)
    sin_t = np.concatenate([0 * ones, sin, sin, np.zeros((rows, tail))], axis=1)
    return tuple(jnp.asarray(a, F32) for a in (cos_t, sin_t, cos.T, sin.T))


def _pad_heads(w, n_heads, per_head, keep_lo, keep_hi):
    k = w.shape[0]
    w = w.reshape(k, n_heads, per_head)[:, :, keep_lo:keep_hi]
    w = jnp.pad(w, ((0, 0), (0, 0), (0, LANES - (keep_hi - keep_lo))))
    return w.reshape(k, n_heads * LANES)


def kernel(x, meta_tokens, ln_in_g, ln_in_b, w_in, fox_f_bias, fox_w_o, mla_q_norm, mla_w_uq,
           mla_kv_norm, mla_w_ukv, mla_w_o, w_out, ln1_g, ln1_b, router_w, router_b,
           w_gate, w_up, w_down, ln2_g, ln2_b):
    batch, seq, _ = x.shape
    depth = w_in.shape[0]
    assert seq % T == 0 and batch % META_PER_TILE == 0
    tpb = seq // T
    n_real = batch * tpb
    n_tiles = n_real + batch // META_PER_TILE
    assert (n_real * T) % TM_MOE == 0 and (n_tiles * T) % TM_MOE == 0
    alpha = (2 * depth) ** 0.25
    row = lambda a: a.reshape(1, -1).astype(F32)

    h = x.reshape(batch * seq, D_MODEL)
    ln_in = (meta_tokens.astype(F32), row(ln_in_g), row(ln_in_b))
    tables = _rope_tables(seq)
    pqt, pk = _decay_placement()
    rw_t = router_w.T.astype(F32)
    rw_hi = rw_t.astype(BF16)
    rw_lo = (rw_t - rw_hi.astype(F32)).astype(BF16)
    rb = router_b.reshape(N_EXPERTS, 1).astype(F32)

    o_q = FOX_WIDTH
    o_k = o_q + FOX_WIDTH
    o_v = o_k + FOX_WIDTH
    o_f = o_v + FOX_HEADS
    o_cq = o_f + MLA_Q_RANK
    o_ckv = o_cq + MLA_KV_RANK
    o_kr = o_ckv + MLA_ROPE_DIM
    for i in range(depth):
        w = w_in[i]
        zeros = lambda n: jnp.zeros((D_MODEL, n), w.dtype)
        w_misc = jnp.concatenate([w[:, o_v:o_f], zeros(KR_LANE - FOX_HEADS), w[:, o_ckv:o_kr],
                                  zeros(LANES - KR_LANE - MLA_ROPE_DIM)], axis=1)
        w_k = _pad_heads(w[:, o_q:o_k], FOX_HEADS, FOX_HEAD_DIM, 0, FOX_HEAD_DIM)
        w_big = jnp.concatenate([w_k, w[:, o_f:o_ckv], w[:, o_kr:], w_misc], axis=1).astype(BF16)
        wqt = w[:, :o_q].T.astype(BF16)
        wvt = w[:, o_k:o_v].T.astype(BF16)
        bias_row = jnp.pad(fox_f_bias[i].astype(F32), (0, LANES - FOX_HEADS)).reshape(1, LANES)
        qk_dim = MLA_NOPE_DIM + MLA_ROPE_DIM
        wuqt = _pad_heads(mla_w_uq[i], MLA_HEADS, qk_dim, 0, qk_dim).T.astype(BF16)
        kv_dim = MLA_NOPE_DIM + MLA_V_DIM
        wuk = _pad_heads(mla_w_ukv[i], MLA_HEADS, kv_dim, 0, MLA_NOPE_DIM).astype(BF16)
        wuv = mla_w_ukv[i].reshape(MLA_KV_RANK, MLA_HEADS, kv_dim)[:, :, MLA_NOPE_DIM:]
        wuvt = wuv.reshape(MLA_KV_RANK, MLA_WIDTH).T.astype(BF16)
        outs = _proj(h, wqt, wvt, w_big, pqt, pk, bias_row, tables, row(mla_q_norm[i]),
                     row(mla_kv_norm[i]), wuqt, wuk, wuvt, n_real, tpb, n_tiles,
                     ln_in=ln_in if i == 0 else None)
        qt_f, k_f, vt_f, g, qt_m, k_m, vt_m = outs[:7]
        if i == 0:
            h = outs[7]
        o_fox = _attention(qt_f, k_f, vt_f, batch, n_real, tpb, "fox_attn")
        o_mla = _attention(qt_m, k_m, vt_m, batch, n_real, tpb, "mla_attn")

        h1, comb_t, gid = _post(o_fox, o_mla, g, h, fox_w_o[i].astype(BF16), mla_w_o[i].astype(BF16),
                                w_out[i].astype(BF16), row(ln1_g[i]), row(ln1_b[i]), rw_hi, rw_lo, rb,
                                alpha)

        wg, wu, wd = w_gate[i].astype(BF16), w_up[i].astype(BF16), w_down[i].astype(BF16)
        moe_rows = (n_real if i == depth - 1 else n_tiles) * T
        h = _moe(h1, comb_t, gid, wg, wu, wd, row(ln2_g[i]), row(ln2_b[i]), alpha, moe_rows // TM_MOE)

    return h.reshape(batch, seq, D_MODEL)
```
